```python
import jax, jax.numpy as jnp
from jax import lax
import numpy as np

D_MODEL = 2048
BATCH = 1
SEQ = 16384
DEPTH = 1
DEC_BATCH = 4
DEC_SEQ = 2048
PAST_LEN = 128

ATTN_WIDTH = D_MODEL // 2
GMLP_WIDTH = D_MODEL - ATTN_WIDTH
HEAD_DIM = 128
N_HEADS = ATTN_WIDTH // HEAD_DIM
N_KV_HEADS = max(1, N_HEADS // 4)
GQA_GROUP = N_HEADS // N_KV_HEADS
GMLP_GROUPS = 8
GMLP_GROUP_DIM = GMLP_WIDTH // GMLP_GROUPS
CHUNK = 128
BLOCK = 128
WINDOW = 128
ROT_DIM = HEAD_DIM // 4
ROPE_THETA = 500000.0
D_FF = ((8 * D_MODEL // 3 + 255) // 256) * 256
EPS = 1e-6
NEG_INF = -1e30
IN_COLS = 2 * GMLP_WIDTH + N_HEADS * HEAD_DIM + 2 * N_KV_HEADS * HEAD_DIM

kernel_name = "hymba_gmlp_swa_macaron_encoder"


def rmsnorm(x, g):
    xf = x.astype(jnp.float32)
    xf = xf * lax.rsqrt(jnp.mean(xf * xf, axis=-1, keepdims=True) + EPS)
    return xf.astype(x.dtype) * g


def swiglu(x, w_gate, w_up, w_down):
    return (jax.nn.silu(x @ w_gate) * (x @ w_up)) @ w_down


def partial_rope(x):
    S = x.shape[1]
    half = ROT_DIM // 2
    inv_freq = ROPE_THETA ** (-jnp.arange(0, ROT_DIM, 2, dtype=jnp.float32) / ROT_DIM)
    ang = jnp.arange(S, dtype=jnp.float32)[:, None] * inv_freq[None, :]
    cos = jnp.cos(ang)[None, :, None, :]
    sin = jnp.sin(ang)[None, :, None, :]
    xr = x[..., :ROT_DIM].astype(jnp.float32)
    x1, x2 = xr[..., :half], xr[..., half:]
    rot = jnp.concatenate([x1 * cos - x2 * sin, x2 * cos + x1 * sin], axis=-1)
    return jnp.concatenate([rot.astype(x.dtype), x[..., ROT_DIM:]], axis=-1)


def chunked_spatial_gating(u, v, v_gain, w_s, b_s):
    B, S, _ = u.shape
    nc = S // CHUNK
    v = rmsnorm(v, v_gain)
    vb = v.reshape(B, nc, CHUNK, GMLP_GROUPS, GMLP_GROUP_DIM)
    mixed = jnp.einsum('gpq,bnqgc->bnpgc', w_s, vb) + b_s.T[None, None, :, :, None]
    return u * mixed.reshape(B, S, GMLP_WIDTH)


def windowed_gqa(q, k, v, sink):
    B, S, _, _ = q.shape
    nb = S // BLOCK
    qb = q.reshape(B, nb, BLOCK, N_KV_HEADS, GQA_GROUP, HEAD_DIM)

    def neighbours(t):
        tp = jnp.pad(t, ((0, 0), (BLOCK, BLOCK), (0, 0), (0, 0)))
        tb = tp.reshape(B, nb + 2, BLOCK, N_KV_HEADS, HEAD_DIM)
        return jnp.concatenate([tb[:, :-2], tb[:, 1:-1], tb[:, 2:]], axis=2)

    kb = neighbours(k)
    vb = neighbours(v)
    scale = HEAD_DIM ** -0.5
    scores = jnp.einsum('bnqkgd,bnjkd->bnkgqj', qb, kb).astype(jnp.float32) * scale
    blk = jnp.arange(nb)[:, None] * BLOCK
    qpos = blk + jnp.arange(BLOCK)[None, :]
    kpos = blk - BLOCK + jnp.arange(3 * BLOCK)[None, :]
    rel = kpos[:, None, :] - qpos[:, :, None]
    valid = (jnp.abs(rel) <= WINDOW) & (kpos[:, None, :] >= 0) & (kpos[:, None, :] < S)
    scores = jnp.where(valid[None, :, None, None], scores, NEG_INF)
    sink_l = sink.astype(jnp.float32).reshape(N_KV_HEADS, GQA_GROUP)[None, None, :, :, None, None]
    sink_b = jnp.broadcast_to(sink_l, scores.shape[:-1] + (1,))
    probs = jax.nn.softmax(jnp.concatenate([scores, sink_b], axis=-1), axis=-1)[..., :-1]
    out = jnp.einsum('bnkgqj,bnjkd->bnqkgd', probs.astype(v.dtype), vb)
    return out.reshape(B, S, N_HEADS * HEAD_DIM)


def encoder_layer(x, ffn1_norm, ffn1_w_gate, ffn1_w_up, ffn1_w_down, mix_norm, w_in,
                  gmlp_v_norm, gmlp_w_s, gmlp_b_s, attn_sink, out_norm_gmlp, out_norm_attn,
                  w_out, ffn2_norm, ffn2_w_gate, ffn2_w_up, ffn2_w_down):
    B, S, _ = x.shape
    h = x + 0.5 * swiglu(rmsnorm(x, ffn1_norm), ffn1_w_gate, ffn1_w_up, ffn1_w_down)
    z = rmsnorm(h, mix_norm) @ w_in
    c1 = GMLP_WIDTH
    c2 = c1 + GMLP_WIDTH
    c3 = c2 + N_HEADS * HEAD_DIM
    c4 = c3 + N_KV_HEADS * HEAD_DIM
    u = jax.nn.gelu(z[..., :c1])
    vg = jax.nn.gelu(z[..., c1:c2])
    q = partial_rope(z[..., c2:c3].reshape(B, S, N_HEADS, HEAD_DIM))
    k = partial_rope(z[..., c3:c4].reshape(B, S, N_KV_HEADS, HEAD_DIM))
    va = z[..., c4:].reshape(B, S, N_KV_HEADS, HEAD_DIM)
    a_out = chunked_spatial_gating(u, vg, gmlp_v_norm, gmlp_w_s, gmlp_b_s)
    b_out = windowed_gqa(q, k, va, attn_sink)
    merged = jnp.concatenate([rmsnorm(a_out, out_norm_gmlp), rmsnorm(b_out, out_norm_attn)], axis=-1)
    h = h + merged @ w_out
    h = h + 0.5 * swiglu(rmsnorm(h, ffn2_norm), ffn2_w_gate, ffn2_w_up, ffn2_w_down)
    return h


def setup_inputs(seed: int = 0) -> dict:
    key = jax.random.key(seed)
    ks = jax.random.split(key, 24)
    f32 = jnp.float32

    def nrm(k, shape, scale):
        return jax.random.normal(k, shape, f32) * scale

    def gain(k, shape):
        return 1.0 + 0.02 * jax.random.normal(k, shape, f32)

    L = DEPTH
    return {
        "x_prompt": jax.random.normal(ks[0], (BATCH, SEQ, D_MODEL), f32),
        "x_sample": jax.random.normal(ks[1], (DEC_BATCH, DEC_SEQ, D_MODEL), f32),
        "ffn1_norm": gain(ks[2], (L, D_MODEL)),
        "ffn1_w_gate": nrm(ks[3], (L, D_MODEL, D_FF), D_MODEL ** -0.5),
        "ffn1_w_up": nrm(ks[4], (L, D_MODEL, D_FF), D_MODEL ** -0.5),
        "ffn1_w_down": nrm(ks[5], (L, D_FF, D_MODEL), D_FF ** -0.5),
        "mix_norm": gain(ks[6], (L, D_MODEL)),
        "w_in": nrm(ks[7], (L, D_MODEL, IN_COLS), D_MODEL ** -0.5),
        "gmlp_v_norm": gain(ks[8], (L, GMLP_WIDTH)),
        "gmlp_w_s": nrm(ks[9], (L, GMLP_GROUPS, CHUNK, CHUNK), CHUNK ** -0.5),
        "gmlp_b_s": nrm(ks[10], (L, GMLP_GROUPS, CHUNK), 0.02),
        "attn_sink": nrm(ks[11], (L, N_HEADS), 0.5),
        "out_norm_gmlp": gain(ks[12], (L, GMLP_WIDTH)),
        "out_norm_attn": gain(ks[13], (L, ATTN_WIDTH)),
        "w_out": nrm(ks[14], (L, GMLP_WIDTH + ATTN_WIDTH, D_MODEL), D_MODEL ** -0.5),
        "ffn2_norm": gain(ks[15], (L, D_MODEL)),
        "ffn2_w_gate": nrm(ks[16], (L, D_MODEL, D_FF), D_MODEL ** -0.5),
        "ffn2_w_up": nrm(ks[17], (L, D_MODEL, D_FF), D_MODEL ** -0.5),
        "ffn2_w_down": nrm(ks[18], (L, D_FF, D_MODEL), D_FF ** -0.5),
        "final_norm": gain(ks[19], (D_MODEL,)),
    }


def reference(x_prompt, x_sample, ffn1_norm, ffn1_w_gate, ffn1_w_up, ffn1_w_down, mix_norm, w_in,
              gmlp_v_norm, gmlp_w_s, gmlp_b_s, attn_sink, out_norm_gmlp, out_norm_attn, w_out,
              ffn2_norm, ffn2_w_gate, ffn2_w_up, ffn2_w_down, final_norm):
    def trunk(x):
        h = x
        for l in range(DEPTH):
            h = encoder_layer(h, ffn1_norm[l], ffn1_w_gate[l], ffn1_w_up[l], ffn1_w_down[l],
                              mix_norm[l], w_in[l], gmlp_v_norm[l], gmlp_w_s[l], gmlp_b_s[l],
                              attn_sink[l], out_norm_gmlp[l], out_norm_attn[l], w_out[l],
                              ffn2_norm[l], ffn2_w_gate[l], ffn2_w_up[l], ffn2_w_down[l])
        return rmsnorm(h, final_norm)

    y_prompt = trunk(x_prompt)
    y_sample = trunk(x_sample)
    return (y_prompt, y_sample)
```

```python
import functools

import jax
import jax.numpy as jnp
from jax import lax
from jax.experimental import pallas as pl
from jax.experimental.pallas import tpu as pltpu

HEAD_DIM = 128
GQA_GROUP = 4
GMLP_GROUPS = 8
CHUNK = 128
ROT_DIM = HEAD_DIM // 4
ROPE_THETA = 500000.0
EPS = 1e-6
NEG_INF = -1e30

VMEM_LIMIT_BYTES = 56 * 1024 * 1024


def _rms(x, gain):
    return x * lax.rsqrt(jnp.mean(x * x, axis=-1, keepdims=True) + EPS) * gain


def _ffn_kernel(x_ref, g_ref, wg_ref, wu_ref, wd_ref, fg_ref, o_ref, n_ref, *, final_norm):
    j = pl.program_id(1)

    @pl.when(j == 0)
    def _():
        n_ref[...] = _rms(x_ref[...], g_ref[...]).astype(jnp.bfloat16)
        o_ref[...] = jnp.zeros_like(o_ref)

    n = n_ref[...]
    gate = jnp.dot(n, wg_ref[...], preferred_element_type=jnp.float32)
    up = jnp.dot(n, wu_ref[...], preferred_element_type=jnp.float32)
    act = (jax.nn.silu(gate) * up).astype(jnp.bfloat16)
    o_ref[...] += jnp.dot(act, wd_ref[...], preferred_element_type=jnp.float32)

    @pl.when(j == pl.num_programs(1) - 1)
    def _():
        h = x_ref[...] + 0.5 * o_ref[...]
        if final_norm:
            h = _rms(h, fg_ref[...])
        o_ref[...] = h


def _ffn(x, gain, wg, wu, wd, fgain, *, final_norm, tm, tf):
    t, d = x.shape
    dff = wg.shape[1]
    return pl.pallas_call(
        functools.partial(_ffn_kernel, final_norm=final_norm),
        grid=(t // tm, dff // tf),
        in_specs=[
            pl.BlockSpec((tm, d), lambda i, j: (i, 0)),
            pl.BlockSpec((1, d), lambda i, j: (0, 0)),
            pl.BlockSpec((d, tf), lambda i, j: (0, j)),
            pl.BlockSpec((d, tf), lambda i, j: (0, j)),
            pl.BlockSpec((tf, d), lambda i, j: (j, 0)),
            pl.BlockSpec((1, d), lambda i, j: (0, 0)),
        ],
        out_specs=pl.BlockSpec((tm, d), lambda i, j: (i, 0)),
        out_shape=jax.ShapeDtypeStruct((t, d), jnp.float32),
        scratch_shapes=[pltpu.VMEM((tm, d), jnp.bfloat16)],
        compiler_params=pltpu.CompilerParams(
            dimension_semantics=("arbitrary", "arbitrary"),
            vmem_limit_bytes=VMEM_LIMIT_BYTES),
        name="ffn_final" if final_norm else "ffn",
    )(x, gain, wg, wu, wd, fgain)


def _rope(x, cos, sin, low_half):
    half = ROT_DIM // 2
    partner = jnp.where(low_half, pltpu.roll(x, HEAD_DIM - half, 1), pltpu.roll(x, half, 1))
    return x * cos + partner * sin


def _inproj_kernel(h_ref, g_ref, w_ref, vg_ref, cos_ref, sin_ref,
                   u_ref, vn_ref, q_ref, k_ref, v_ref, *, gw, qw, kw):
    n = _rms(h_ref[...], g_ref[...]).astype(jnp.bfloat16)
    c1, c2, c3, c4 = gw, 2 * gw, 2 * gw + qw, 2 * gw + qw + kw

    def proj(lo, hi):
        return jnp.dot(n, w_ref[:, lo:hi], preferred_element_type=jnp.float32)

    u_ref[...] = jax.nn.gelu(proj(0, c1)).astype(jnp.bfloat16)
    vn_ref[...] = _rms(jax.nn.gelu(proj(c1, c2)), vg_ref[...]).astype(jnp.bfloat16)

    cos = cos_ref[...]
    sin = sin_ref[...]
    low_half = lax.broadcasted_iota(jnp.int32, cos.shape, 1) < (ROT_DIM // 2)
    zq = proj(c2, c3)
    for hd in range(qw // HEAD_DIM):
        sl = slice(hd * HEAD_DIM, (hd + 1) * HEAD_DIM)
        q_ref[:, sl] = _rope(zq[:, sl], cos, sin, low_half).astype(jnp.bfloat16)
    zk = proj(c3, c4)
    for hd in range(kw // HEAD_DIM):
        sl = slice(hd * HEAD_DIM, (hd + 1) * HEAD_DIM)
        k_ref[:, sl] = _rope(zk[:, sl], cos, sin, low_half).astype(jnp.bfloat16)
    v_ref[...] = proj(c4, c4 + kw).astype(jnp.bfloat16)


def _inproj(h, gain, w_in, vgain, cos, sin, *, seq, tm):
    t, d = h.shape
    gw = vgain.shape[1]
    qw = d - gw
    kw = qw // GQA_GROUP
    cols = w_in.shape[1]
    tiles_per_seq = seq // tm
    bf = jnp.bfloat16
    return pl.pallas_call(
        functools.partial(_inproj_kernel, gw=gw, qw=qw, kw=kw),
        grid=(t // tm,),
        in_specs=[
            pl.BlockSpec((tm, d), lambda i: (i, 0)),
            pl.BlockSpec((1, d), lambda i: (0, 0)),
            pl.BlockSpec((d, cols), lambda i: (0, 0), pipeline_mode=pl.Buffered(1)),
            pl.BlockSpec((1, gw), lambda i: (0, 0)),
            pl.BlockSpec((tm, HEAD_DIM), lambda i: (i % tiles_per_seq, 0)),
            pl.BlockSpec((tm, HEAD_DIM), lambda i: (i % tiles_per_seq, 0)),
        ],
        out_specs=[
            pl.BlockSpec((tm, gw), lambda i: (i, 0)),
            pl.BlockSpec((tm, gw), lambda i: (i, 0)),
            pl.BlockSpec((tm, qw), lambda i: (i, 0)),
            pl.BlockSpec((tm, kw), lambda i: (i, 0)),
            pl.BlockSpec((tm, kw), lambda i: (i, 0)),
        ],
        out_shape=[
            jax.ShapeDtypeStruct((t, gw), bf),
            jax.ShapeDtypeStruct((t, gw), bf),
            jax.ShapeDtypeStruct((t, qw), bf),
            jax.ShapeDtypeStruct((t, kw), bf),
            jax.ShapeDtypeStruct((t, kw), bf),
        ],
        compiler_params=pltpu.CompilerParams(
            dimension_semantics=("arbitrary",),
            vmem_limit_bytes=VMEM_LIMIT_BYTES),
        name="inproj",
    )(h, gain, w_in, vgain, cos, sin)


def _mixer_kernel(sink_ref, h_ref, u_ref, vn_ref, q_ref, k_ref, v_ref,
                  kp_ref, vp_ref, kn_ref, vn2_ref, ws_ref, bs_ref,
                  ga_ref, gb_ref, wo_ref, o_ref,
                  a_ref, m_ref, kall_ref, vall_ref, *, tiles_per_seq, nblk):
    i = pl.program_id(0)
    gw = u_ref.shape[1]
    n_kv = k_ref.shape[1] // HEAD_DIM
    rows = GQA_GROUP * CHUNK

    for g in range(GMLP_GROUPS):
        gs = slice(g * CHUNK, (g + 1) * CHUNK)
        rhs = jnp.concatenate(
            [vn_ref[c * CHUNK:(c + 1) * CHUNK, gs] for c in range(nblk)], axis=1)
        mixed = jnp.dot(ws_ref[g], rhs, preferred_element_type=jnp.float32)
        for c in range(nblk):
            cs = slice(c * CHUNK, (c + 1) * CHUNK)
            a_ref[cs, gs] = u_ref[cs, gs].astype(jnp.float32) * (mixed[:, cs] + bs_ref[g])
    m_ref[:, :gw] = _rms(a_ref[...], ga_ref[...]).astype(jnp.bfloat16)

    kall_ref[0:CHUNK] = kp_ref[...]
    kall_ref[CHUNK:(nblk + 1) * CHUNK] = k_ref[...]
    kall_ref[(nblk + 1) * CHUNK:] = kn_ref[...]
    vall_ref[0:CHUNK] = vp_ref[...]
    vall_ref[CHUNK:(nblk + 1) * CHUNK] = v_ref[...]
    vall_ref[(nblk + 1) * CHUNK:] = vn2_ref[...]

    t_in_seq = i % tiles_per_seq
    prev_thr = jnp.where(t_in_seq == 0, 2 * CHUNK, 0)
    next_thr = jnp.where(t_in_seq == tiles_per_seq - 1, -2 * CHUNK, 0)
    rel = (lax.broadcasted_iota(jnp.int32, (rows, CHUNK), 1)
           - (lax.broadcasted_iota(jnp.int32, (rows, CHUNK), 0) & (CHUNK - 1)))
    scale = HEAD_DIM ** -0.5
    for b in range(nblk):
        pthr = prev_thr if b == 0 else 0
        nthr = next_thr if b == nblk - 1 else 0
        qs = slice(b * CHUNK, (b + 1) * CHUNK)
        for kh in range(n_kv):
            ks = slice(kh * HEAD_DIM, (kh + 1) * HEAD_DIM)
            q = jnp.concatenate(
                [q_ref[qs, (kh * GQA_GROUP + g) * HEAD_DIM:(kh * GQA_GROUP + g + 1) * HEAD_DIM]
                 for g in range(GQA_GROUP)], axis=0)
            kwin = kall_ref[b * CHUNK:(b + 3) * CHUNK, ks]
            vwin = vall_ref[b * CHUNK:(b + 3) * CHUNK, ks]
            s = lax.dot_general(q, kwin, (((1,), (1,)), ((), ())),
                                preferred_element_type=jnp.float32) * scale
            s_prev = jnp.where(rel >= pthr, s[:, :CHUNK], NEG_INF)
            s_mid = s[:, CHUNK:2 * CHUNK]
            s_next = jnp.where(rel <= nthr, s[:, 2 * CHUNK:], NEG_INF)
            sink = jnp.concatenate(
                [jnp.full((CHUNK, 1), sink_ref[kh * GQA_GROUP + g], jnp.float32)
                 for g in range(GQA_GROUP)], axis=0)
            mx = jnp.maximum(
                jnp.max(jnp.maximum(jnp.maximum(s_prev, s_mid), s_next), axis=-1, keepdims=True),
                sink)
            p_prev = jnp.exp(s_prev - mx)
            p_mid = jnp.exp(s_mid - mx)
            p_next = jnp.exp(s_next - mx)
            denom = (jnp.sum(p_prev + p_mid + p_next, axis=-1, keepdims=True)
                     + jnp.exp(sink - mx))
            p = jnp.concatenate([p_prev, p_mid, p_next], axis=1).astype(jnp.bfloat16)
            o = jnp.dot(p, vwin, preferred_element_type=jnp.float32) / denom
            for g in range(GQA_GROUP):
                hs = (kh * GQA_GROUP + g) * HEAD_DIM
                a_ref[qs, hs:hs + HEAD_DIM] = o[g * CHUNK:(g + 1) * CHUNK]
    m_ref[:, gw:] = _rms(a_ref[...], gb_ref[...]).astype(jnp.bfloat16)

    o_ref[...] = h_ref[...] + jnp.dot(m_ref[...], wo_ref[...], preferred_element_type=jnp.float32)


def _mixer(h, u, vn, q, k, v, sink, ws, bs, ga, gb, wo, *, seq, tm):
    t, d = h.shape
    gw = u.shape[1]
    qw = q.shape[1]
    kw = k.shape[1]
    nblk = tm // CHUNK
    nblocks = t // CHUNK
    tiles_per_seq = seq // tm
    row = lambda i: (i, 0)
    const2 = lambda i: (0, 0)
    const3 = lambda i: (0, 0, 0)
    prev_blk = lambda i: (jnp.maximum(i * nblk - 1, 0), 0)
    next_blk = lambda i: (jnp.minimum((i + 1) * nblk, nblocks - 1), 0)
    return pl.pallas_call(
        functools.partial(_mixer_kernel, tiles_per_seq=tiles_per_seq, nblk=nblk),
        grid=(t // tm,),
        in_specs=[
            pl.BlockSpec(memory_space=pltpu.SMEM),
            pl.BlockSpec((tm, d), row),
            pl.BlockSpec((tm, gw), row),
            pl.BlockSpec((tm, gw), row),
            pl.BlockSpec((tm, qw), row),
            pl.BlockSpec((tm, kw), row),
            pl.BlockSpec((tm, kw), row),
            pl.BlockSpec((CHUNK, kw), prev_blk),
            pl.BlockSpec((CHUNK, kw), prev_blk),
            pl.BlockSpec((CHUNK, kw), next_blk),
            pl.BlockSpec((CHUNK, kw), next_blk),
            pl.BlockSpec((GMLP_GROUPS, CHUNK, CHUNK), const3),
            pl.BlockSpec((GMLP_GROUPS, CHUNK, CHUNK), const3),
            pl.BlockSpec((1, gw), const2),
            pl.BlockSpec((1, qw), const2),
            pl.BlockSpec((d, d), const2, pipeline_mode=pl.Buffered(1)),
        ],
        out_specs=pl.BlockSpec((tm, d), row),
        out_shape=jax.ShapeDtypeStruct((t, d), jnp.float32),
        scratch_shapes=[
            pltpu.VMEM((tm, gw), jnp.float32),
            pltpu.VMEM((tm, d), jnp.bfloat16),
            pltpu.VMEM((tm + 2 * CHUNK, kw), jnp.bfloat16),
            pltpu.VMEM((tm + 2 * CHUNK, kw), jnp.bfloat16),
        ],
        compiler_params=pltpu.CompilerParams(
            dimension_semantics=("arbitrary",),
            vmem_limit_bytes=VMEM_LIMIT_BYTES),
        name="mixer",
    )(sink, h, u, vn, q, k, v, k, v, k, v, ws, bs, ga, gb, wo)


def _rope_tables(seq):
    half = ROT_DIM // 2
    inv_freq = ROPE_THETA ** (-jnp.arange(0, ROT_DIM, 2, dtype=jnp.float32) / ROT_DIM)
    ang = jnp.arange(seq, dtype=jnp.float32)[:, None] * inv_freq[None, :]
    cos = jnp.cos(ang)
    sin = jnp.sin(ang)
    pad = HEAD_DIM - ROT_DIM
    cos_t = jnp.concatenate([cos, cos, jnp.ones((seq, pad), jnp.float32)], axis=1)
    sin_t = jnp.concatenate([-sin, sin, jnp.zeros((seq, pad), jnp.float32)], axis=1)
    return cos_t, sin_t


def kernel(x_prompt, x_sample, ffn1_norm, ffn1_w_gate, ffn1_w_up, ffn1_w_down, mix_norm, w_in,
           gmlp_v_norm, gmlp_w_s, gmlp_b_s, attn_sink, out_norm_gmlp, out_norm_attn, w_out,
           ffn2_norm, ffn2_w_gate, ffn2_w_up, ffn2_w_down, final_norm):
    bf = jnp.bfloat16
    depth = ffn1_norm.shape[0]
    d = x_prompt.shape[-1]
    row = lambda a: a.reshape(1, -1)
    layers = []
    for l in range(depth):
        layers.append(dict(
            g1=row(ffn1_norm[l]), wg1=ffn1_w_gate[l].astype(bf), wu1=ffn1_w_up[l].astype(bf),
            wd1=ffn1_w_down[l].astype(bf),
            gm=row(mix_norm[l]), w_in=w_in[l].astype(bf), gv=row(gmlp_v_norm[l]),
            ws=gmlp_w_s[l].astype(bf),
            bs=jnp.broadcast_to(gmlp_b_s[l][:, :, None], gmlp_b_s[l].shape + (CHUNK,)),
            sink=attn_sink[l], ga=row(out_norm_gmlp[l]), gb=row(out_norm_attn[l]),
            wo=w_out[l].astype(bf),
            g2=row(ffn2_norm[l]), wg2=ffn2_w_gate[l].astype(bf), wu2=ffn2_w_up[l].astype(bf),
            wd2=ffn2_w_down[l].astype(bf)))
    fg = row(final_norm)

    def trunk(x):
        b, seq, _ = x.shape
        h = x.reshape(b * seq, d)
        cos, sin = _rope_tables(seq)
        for l, p in enumerate(layers):
            last = l == depth - 1
            h = _ffn(h, p["g1"], p["wg1"], p["wu1"], p["wd1"], fg, final_norm=False, tm=512, tf=512)
            u, vn, q, k, v = _inproj(h, p["gm"], p["w_in"], p["gv"], cos, sin, seq=seq, tm=512)
            h = _mixer(h, u, vn, q, k, v, p["sink"], p["ws"], p["bs"], p["ga"], p["gb"], p["wo"],
                       seq=seq, tm=512)
            h = _ffn(h, p["g2"], p["wg2"], p["wu2"], p["wd2"], fg, final_norm=last, tm=512, tf=512)
        return h.reshape(b, seq, d)

    return (trunk(x_prompt), trunk(x_sample))
```

```python
import functools

import jax
import jax.numpy as jnp
from jax import lax
from jax.experimental import pallas as pl
from jax.experimental.pallas import tpu as pltpu

HEAD_DIM = 128
GQA_GROUP = 4
GMLP_GROUPS = 8
CHUNK = 128
ROT_DIM = HEAD_DIM // 4
ROPE_THETA = 500000.0
EPS = 1e-6
NEG_INF = -1e30

VMEM_LIMIT_BYTES = 60 * 1024 * 1024

FFN_TM = 1024
FFN_TF = 512
FFN_ROW_CHUNK = 128
INPROJ_TM = 512
MIXER_TM = 512


def _rms(x, gain):
    return x * lax.rsqrt(jnp.mean(x * x, axis=-1, keepdims=True) + EPS) * gain


def _ffn_kernel(x_ref, g_ref, wg_ref, wu_ref, wd_ref, fg_ref, o_ref, n_ref, *, final_norm):
    j = pl.program_id(1)

    n_chunks = x_ref.shape[0] // FFN_ROW_CHUNK

    def rows(r):
        return pl.ds(pl.multiple_of(r * FFN_ROW_CHUNK, FFN_ROW_CHUNK), FFN_ROW_CHUNK)

    @pl.when(j == 0)
    def _():
        def norm_chunk(r, carry):
            n_ref[rows(r), :] = _rms(x_ref[rows(r), :], g_ref[...]).astype(jnp.bfloat16)
            o_ref[rows(r), :] = jnp.zeros((FFN_ROW_CHUNK, o_ref.shape[1]), jnp.float32)
            return carry
        lax.fori_loop(0, n_chunks, norm_chunk, 0)

    n = n_ref[...]
    gate = jnp.dot(n, wg_ref[...], preferred_element_type=jnp.float32)
    up = jnp.dot(n, wu_ref[...], preferred_element_type=jnp.float32)
    act = (jax.nn.silu(gate) * up).astype(jnp.bfloat16)
    o_ref[...] += jnp.dot(act, wd_ref[...], preferred_element_type=jnp.float32)

    @pl.when(j == pl.num_programs(1) - 1)
    def _():
        def residual_chunk(r, carry):
            h = x_ref[rows(r), :] + 0.5 * o_ref[rows(r), :]
            if final_norm:
                h = _rms(h, fg_ref[...])
            o_ref[rows(r), :] = h
            return carry
        lax.fori_loop(0, n_chunks, residual_chunk, 0)


def _ffn(x, gain, wg, wu, wd, fgain, *, final_norm, tm, tf):
    t, d = x.shape
    dff = wg.shape[1]
    return pl.pallas_call(
        functools.partial(_ffn_kernel, final_norm=final_norm),
        grid=(t // tm, dff // tf),
        in_specs=[
            pl.BlockSpec((tm, d), lambda i, j: (i, 0)),
            pl.BlockSpec((1, d), lambda i, j: (0, 0)),
            pl.BlockSpec((d, tf), lambda i, j: (0, j)),
            pl.BlockSpec((d, tf), lambda i, j: (0, j)),
            pl.BlockSpec((tf, d), lambda i, j: (j, 0)),
            pl.BlockSpec((1, d), lambda i, j: (0, 0)),
        ],
        out_specs=pl.BlockSpec((tm, d), lambda i, j: (i, 0)),
        out_shape=jax.ShapeDtypeStruct((t, d), jnp.float32),
        scratch_shapes=[pltpu.VMEM((tm, d), jnp.bfloat16)],
        compiler_params=pltpu.CompilerParams(
            dimension_semantics=("arbitrary", "arbitrary"),
            vmem_limit_bytes=VMEM_LIMIT_BYTES),
        name="ffn_final" if final_norm else "ffn",
    )(x, gain, wg, wu, wd, fgain)


def _rope(x, cos, sin, low_half):
    half = ROT_DIM // 2
    partner = jnp.where(low_half, pltpu.roll(x, HEAD_DIM - half, 1), pltpu.roll(x, half, 1))
    return x * cos + partner * sin


def _inproj_kernel(h_ref, g_ref, w_ref, vg_ref, cos_ref, sin_ref,
                   u_ref, vn_ref, q_ref, k_ref, v_ref, *, gw, qw, kw):
    n = _rms(h_ref[...], g_ref[...]).astype(jnp.bfloat16)
    c1, c2, c3, c4 = gw, 2 * gw, 2 * gw + qw, 2 * gw + qw + kw

    def proj(lo, hi):
        return jnp.dot(n, w_ref[:, lo:hi], preferred_element_type=jnp.float32)

    u_ref[...] = jax.nn.gelu(proj(0, c1)).astype(jnp.bfloat16)
    vn_ref[...] = _rms(jax.nn.gelu(proj(c1, c2)), vg_ref[...]).astype(jnp.bfloat16)

    cos = cos_ref[...]
    sin = sin_ref[...]
    low_half = lax.broadcasted_iota(jnp.int32, cos.shape, 1) < (ROT_DIM // 2)
    zq = proj(c2, c3)
    for hd in range(qw // HEAD_DIM):
        sl = slice(hd * HEAD_DIM, (hd + 1) * HEAD_DIM)
        q_ref[:, sl] = _rope(zq[:, sl], cos, sin, low_half).astype(jnp.bfloat16)
    zk = proj(c3, c4)
    for hd in range(kw // HEAD_DIM):
        sl = slice(hd * HEAD_DIM, (hd + 1) * HEAD_DIM)
        k_ref[:, sl] = _rope(zk[:, sl], cos, sin, low_half).astype(jnp.bfloat16)
    v_ref[...] = proj(c4, c4 + kw).astype(jnp.bfloat16)


def _inproj(h, gain, w_in, vgain, cos, sin, *, seq, tm):
    t, d = h.shape
    gw = vgain.shape[1]
    qw = d - gw
    kw = qw // GQA_GROUP
    cols = w_in.shape[1]
    tiles_per_seq = seq // tm
    bf = jnp.bfloat16
    return pl.pallas_call(
        functools.partial(_inproj_kernel, gw=gw, qw=qw, kw=kw),
        grid=(t // tm,),
        in_specs=[
            pl.BlockSpec((tm, d), lambda i: (i, 0)),
            pl.BlockSpec((1, d), lambda i: (0, 0)),
            pl.BlockSpec((d, cols), lambda i: (0, 0), pipeline_mode=pl.Buffered(1)),
            pl.BlockSpec((1, gw), lambda i: (0, 0)),
            pl.BlockSpec((tm, HEAD_DIM), lambda i: (i % tiles_per_seq, 0)),
            pl.BlockSpec((tm, HEAD_DIM), lambda i: (i % tiles_per_seq, 0)),
        ],
        out_specs=[
            pl.BlockSpec((tm, gw), lambda i: (i, 0)),
            pl.BlockSpec((tm, gw), lambda i: (i, 0)),
            pl.BlockSpec((tm, qw), lambda i: (i, 0)),
            pl.BlockSpec((tm, kw), lambda i: (i, 0)),
            pl.BlockSpec((tm, kw), lambda i: (i, 0)),
        ],
        out_shape=[
            jax.ShapeDtypeStruct((t, gw), bf),
            jax.ShapeDtypeStruct((t, gw), bf),
            jax.ShapeDtypeStruct((t, qw), bf),
            jax.ShapeDtypeStruct((t, kw), bf),
            jax.ShapeDtypeStruct((t, kw), bf),
        ],
        compiler_params=pltpu.CompilerParams(
            dimension_semantics=("arbitrary",),
            vmem_limit_bytes=VMEM_LIMIT_BYTES),
        name="inproj",
    )(h, gain, w_in, vgain, cos, sin)


def _mixer_kernel(sink_ref, h_ref, u_ref, vn_ref, q_ref, k_ref, v_ref,
                  kp_ref, vp_ref, kn_ref, vn2_ref, ws_ref, bs_ref,
                  ga_ref, gb_ref, wo_ref, o_ref,
                  a_ref, att_ref, m_ref, mprev_ref, kall_ref, vall_ref,
                  *, ntiles, tiles_per_seq, nblk):
    s = pl.program_id(0)
    gw = u_ref.shape[1]
    n_kv = k_ref.shape[1] // HEAD_DIM
    rows = GQA_GROUP * CHUNK

    @pl.when(s == 0)
    def _():
        m_ref[...] = jnp.zeros_like(m_ref)

    mprev_ref[...] = m_ref[...]

    kall_ref[0:CHUNK] = kp_ref[...]
    kall_ref[CHUNK:(nblk + 1) * CHUNK] = k_ref[...]
    kall_ref[(nblk + 1) * CHUNK:] = kn_ref[...]
    vall_ref[0:CHUNK] = vp_ref[...]
    vall_ref[CHUNK:(nblk + 1) * CHUNK] = v_ref[...]
    vall_ref[(nblk + 1) * CHUNK:] = vn2_ref[...]

    t_in_seq = jnp.minimum(s, ntiles - 1) % tiles_per_seq
    prev_thr = jnp.where(t_in_seq == 0, 2 * CHUNK, 0)
    next_thr = jnp.where(t_in_seq == tiles_per_seq - 1, -2 * CHUNK, 0)
    rel = (lax.broadcasted_iota(jnp.int32, (rows, CHUNK), 1)
           - (lax.broadcasted_iota(jnp.int32, (rows, CHUNK), 0) & (CHUNK - 1)))
    log2e = 1.4426950408889634
    c2 = (HEAD_DIM ** -0.5) * log2e
    pw = o_ref.shape[1] // (nblk * n_kv)
    gpi = GMLP_GROUPS // n_kv
    for b in range(nblk):
        pthr = prev_thr if b == 0 else 0
        nthr = next_thr if b == nblk - 1 else 0
        qs = slice(b * CHUNK, (b + 1) * CHUNK)
        for kh in range(n_kv):
            ks = slice(kh * HEAD_DIM, (kh + 1) * HEAD_DIM)
            for g in range(kh * gpi, (kh + 1) * gpi):
                gs = slice(g * CHUNK, (g + 1) * CHUNK)
                mixed = jnp.dot(ws_ref[g], vn_ref[qs, gs], preferred_element_type=jnp.float32)
                a_ref[qs, gs] = u_ref[qs, gs].astype(jnp.float32) * (mixed + bs_ref[g])
            q = jnp.concatenate(
                [q_ref[qs, (kh * GQA_GROUP + g) * HEAD_DIM:(kh * GQA_GROUP + g + 1) * HEAD_DIM]
                 for g in range(GQA_GROUP)], axis=0)
            kwin = kall_ref[b * CHUNK:(b + 3) * CHUNK, ks]
            vwin = vall_ref[b * CHUNK:(b + 3) * CHUNK, ks]
            sc = lax.dot_general(q, kwin, (((1,), (1,)), ((), ())),
                                 preferred_element_type=jnp.float32)
            ps = slice((b * n_kv + kh) * pw, (b * n_kv + kh + 1) * pw)
            o_ref[:, ps] = h_ref[:, ps] + jnp.dot(mprev_ref[...], wo_ref[:, ps],
                                                  preferred_element_type=jnp.float32)
            s_prev = jnp.where(rel >= pthr, sc[:, :CHUNK], NEG_INF)
            s_mid = sc[:, CHUNK:2 * CHUNK]
            s_next = jnp.where(rel <= nthr, sc[:, 2 * CHUNK:], NEG_INF)
            sink2 = jnp.concatenate(
                [jnp.full((CHUNK, 1), sink_ref[kh * GQA_GROUP + g] * log2e, jnp.float32)
                 for g in range(GQA_GROUP)], axis=0)
            mx = jnp.maximum(
                jnp.max(jnp.maximum(jnp.maximum(s_prev, s_mid), s_next), axis=-1, keepdims=True) * c2,
                sink2)
            p_prev = jnp.exp2(s_prev * c2 - mx)
            p_mid = jnp.exp2(s_mid * c2 - mx)
            p_next = jnp.exp2(s_next * c2 - mx)
            denom = (jnp.sum(p_prev + p_mid + p_next, axis=-1, keepdims=True)
                     + jnp.exp2(sink2 - mx))
            p = jnp.concatenate([p_prev, p_mid, p_next], axis=1).astype(jnp.bfloat16)
            o = jnp.dot(p, vwin, preferred_element_type=jnp.float32) / denom
            for g in range(GQA_GROUP):
                hs = (kh * GQA_GROUP + g) * HEAD_DIM
                att_ref[qs, hs:hs + HEAD_DIM] = o[g * CHUNK:(g + 1) * CHUNK]
        m_ref[qs, :gw] = _rms(a_ref[qs, :], ga_ref[...]).astype(jnp.bfloat16)
        m_ref[qs, gw:] = _rms(att_ref[qs, :], gb_ref[...]).astype(jnp.bfloat16)


def _mixer(h, u, vn, q, k, v, sink, ws, bs, ga, gb, wo, *, seq, tm):
    t, d = h.shape
    gw = u.shape[1]
    qw = q.shape[1]
    kw = k.shape[1]
    nblk = tm // CHUNK
    nblocks = t // CHUNK
    ntiles = t // tm
    tiles_per_seq = seq // tm
    mix_tile = lambda s: jnp.minimum(s, ntiles - 1)
    mix_row = lambda s: (mix_tile(s), 0)
    proj_row = lambda s: (jnp.maximum(s - 1, 0), 0)
    const2 = lambda s: (0, 0)
    const3 = lambda s: (0, 0, 0)
    prev_blk = lambda s: (jnp.maximum(mix_tile(s) * nblk - 1, 0), 0)
    next_blk = lambda s: (jnp.minimum((mix_tile(s) + 1) * nblk, nblocks - 1), 0)
    return pl.pallas_call(
        functools.partial(_mixer_kernel, ntiles=ntiles, tiles_per_seq=tiles_per_seq, nblk=nblk),
        grid=(ntiles + 1,),
        in_specs=[
            pl.BlockSpec(memory_space=pltpu.SMEM),
            pl.BlockSpec((tm, d), proj_row),
            pl.BlockSpec((tm, gw), mix_row),
            pl.BlockSpec((tm, gw), mix_row),
            pl.BlockSpec((tm, qw), mix_row),
            pl.BlockSpec((tm, kw), mix_row),
            pl.BlockSpec((tm, kw), mix_row),
            pl.BlockSpec((CHUNK, kw), prev_blk),
            pl.BlockSpec((CHUNK, kw), prev_blk),
            pl.BlockSpec((CHUNK, kw), next_blk),
            pl.BlockSpec((CHUNK, kw), next_blk),
            pl.BlockSpec((GMLP_GROUPS, CHUNK, CHUNK), const3),
            pl.BlockSpec((GMLP_GROUPS, CHUNK, CHUNK), const3),
            pl.BlockSpec((1, gw), const2),
            pl.BlockSpec((1, qw), const2),
            pl.BlockSpec((d, d), const2, pipeline_mode=pl.Buffered(1)),
        ],
        out_specs=pl.BlockSpec((tm, d), proj_row),
        out_shape=jax.ShapeDtypeStruct((t, d), jnp.float32),
        scratch_shapes=[
            pltpu.VMEM((tm, gw), jnp.float32),
            pltpu.VMEM((tm, qw), jnp.float32),
            pltpu.VMEM((tm, d), jnp.bfloat16),
            pltpu.VMEM((tm, d), jnp.bfloat16),
            pltpu.VMEM((tm + 2 * CHUNK, kw), jnp.bfloat16),
            pltpu.VMEM((tm + 2 * CHUNK, kw), jnp.bfloat16),
        ],
        compiler_params=pltpu.CompilerParams(
            dimension_semantics=("arbitrary",),
            vmem_limit_bytes=VMEM_LIMIT_BYTES),
        name="mixer",
    )(sink, h, u, vn, q, k, v, k, v, k, v, ws, bs, ga, gb, wo)


def _rope_tables(seq):
    half = ROT_DIM // 2
    inv_freq = ROPE_THETA ** (-jnp.arange(0, ROT_DIM, 2, dtype=jnp.float32) / ROT_DIM)
    ang = jnp.arange(seq, dtype=jnp.float32)[:, None] * inv_freq[None, :]
    cos = jnp.cos(ang)
    sin = jnp.sin(ang)
    pad = HEAD_DIM - ROT_DIM
    cos_t = jnp.concatenate([cos, cos, jnp.ones((seq, pad), jnp.float32)], axis=1)
    sin_t = jnp.concatenate([-sin, sin, jnp.zeros((seq, pad), jnp.float32)], axis=1)
    return cos_t, sin_t


def kernel(x_prompt, x_sample, ffn1_norm, ffn1_w_gate, ffn1_w_up, ffn1_w_down, mix_norm, w_in,
           gmlp_v_norm, gmlp_w_s, gmlp_b_s, attn_sink, out_norm_gmlp, out_norm_attn, w_out,
           ffn2_norm, ffn2_w_gate, ffn2_w_up, ffn2_w_down, final_norm):
    bf = jnp.bfloat16
    depth = ffn1_norm.shape[0]
    d = x_prompt.shape[-1]
    row = lambda a: a.reshape(1, -1)
    layers = []
    for l in range(depth):
        layers.append(dict(
            g1=row(ffn1_norm[l]), wg1=ffn1_w_gate[l].astype(bf), wu1=ffn1_w_up[l].astype(bf),
            wd1=ffn1_w_down[l].astype(bf),
            gm=row(mix_norm[l]), w_in=w_in[l].astype(bf), gv=row(gmlp_v_norm[l]),
            ws=gmlp_w_s[l].astype(bf),
            bs=jnp.broadcast_to(gmlp_b_s[l][:, :, None], gmlp_b_s[l].shape + (CHUNK,)),
            sink=attn_sink[l], ga=row(out_norm_gmlp[l]), gb=row(out_norm_attn[l]),
            wo=w_out[l].astype(bf),
            g2=row(ffn2_norm[l]), wg2=ffn2_w_gate[l].astype(bf), wu2=ffn2_w_up[l].astype(bf),
            wd2=ffn2_w_down[l].astype(bf)))
    fg = row(final_norm)

    def trunk(x):
        b, seq, _ = x.shape
        h = x.reshape(b * seq, d)
        cos, sin = _rope_tables(seq)
        for l, p in enumerate(layers):
            last = l == depth - 1
            h = _ffn(h, p["g1"], p["wg1"], p["wu1"], p["wd1"], fg, final_norm=False,
                     tm=FFN_TM, tf=FFN_TF)
            u, vn, q, k, v = _inproj(h, p["gm"], p["w_in"], p["gv"], cos, sin, seq=seq, tm=INPROJ_TM)
            h = _mixer(h, u, vn, q, k, v, p["sink"], p["ws"], p["bs"], p["ga"], p["gb"], p["wo"],
                       seq=seq, tm=MIXER_TM)
            h = _ffn(h, p["g2"], p["wg2"], p["wu2"], p["wd2"], fg, final_norm=last,
                     tm=FFN_TM, tf=FFN_TF)
        return h.reshape(b, seq, d)

    return (trunk(x_prompt), trunk(x_sample))
```

```python
import functools

import jax
import jax.numpy as jnp
from jax import lax
from jax.experimental import pallas as pl
from jax.experimental.pallas import tpu as pltpu

HEAD_DIM = 128
GQA_GROUP = 4
GMLP_GROUPS = 8
CHUNK = 128
ROT_DIM = HEAD_DIM // 4
ROPE_THETA = 500000.0
EPS = 1e-6
NEG_INF = -1e30

VMEM_LIMIT_BYTES = 60 * 1024 * 1024

FFN_TM = 1024
FFN_TF = 512
FFN_ROW_CHUNK = 128
CAST_BLOCK_LONG = 512
INPROJ_TM = 512
MIXER_TM = 512


def _rms(x, gain):
    return x * lax.rsqrt(jnp.mean(x * x, axis=-1, keepdims=True) + EPS) * gain


def _ffn_kernel(*refs, final_norm, n_cast):
    x_ref, g_ref, wg_ref, wu_ref, wd_ref, fg_ref = refs[:6]
    cast_in = refs[6:6 + n_cast]
    o_ref = refs[6 + n_cast]
    cast_out = refs[7 + n_cast:7 + 2 * n_cast]
    n_ref = refs[7 + 2 * n_cast]
    j = pl.program_id(1)

    n_chunks = x_ref.shape[0] // FFN_ROW_CHUNK

    def rows(r):
        return pl.ds(pl.multiple_of(r * FFN_ROW_CHUNK, FFN_ROW_CHUNK), FFN_ROW_CHUNK)

    @pl.when(j == 0)
    def _():
        def norm_chunk(r, carry):
            x = x_ref[rows(r), :]
            n_ref[rows(r), :] = _rms(x, g_ref[...]).astype(jnp.bfloat16)
            o_ref[rows(r), :] = x
            return carry
        lax.fori_loop(0, n_chunks, norm_chunk, 0)

    for src, dst in zip(cast_in, cast_out):
        dst[...] = src[...].astype(dst.dtype)

    n = n_ref[...]
    gate = jnp.dot(n, wg_ref[...], preferred_element_type=jnp.float32)
    up = jnp.dot(n, wu_ref[...], preferred_element_type=jnp.float32)
    act = (0.5 * jax.nn.silu(gate) * up).astype(jnp.bfloat16)
    o_ref[...] += jnp.dot(act, wd_ref[...], preferred_element_type=jnp.float32)

    if final_norm:
        @pl.when(j == pl.num_programs(1) - 1)
        def _():
            def norm_chunk(r, carry):
                o_ref[rows(r), :] = _rms(o_ref[rows(r), :], fg_ref[...])
                return carry
            lax.fori_loop(0, n_chunks, norm_chunk, 0)


def _cast_plan(w, n_row_steps, n_col_steps):
    r, c = w.shape
    transpose = r > c
    assert min(r, c) % n_row_steps == 0 and max(r, c) % CAST_BLOCK_LONG == 0
    assert max(r, c) // CAST_BLOCK_LONG <= n_col_steps
    br = r // n_row_steps if not transpose else CAST_BLOCK_LONG
    bc = CAST_BLOCK_LONG if not transpose else c // n_row_steps
    if transpose:
        nlong = r // br
        return (br, bc), (lambda i, j: (jnp.minimum(j, nlong - 1), i))
    nlong = c // bc
    return (br, bc), (lambda i, j: (i, jnp.minimum(j, nlong - 1)))


def _ffn(x, gain, wg, wu, wd, fgain, *, final_norm, tm, tf, cast=()):
    t, d = x.shape
    dff = wg.shape[1]
    grid = (t // tm, dff // tf)
    plans = [_cast_plan(w, *grid) for w in cast]
    cast_specs = [pl.BlockSpec(blk, imap) for blk, imap in plans]
    outs = pl.pallas_call(
        functools.partial(_ffn_kernel, final_norm=final_norm, n_cast=len(cast)),
        grid=grid,
        in_specs=[
            pl.BlockSpec((tm, d), lambda i, j: (i, 0)),
            pl.BlockSpec((1, d), lambda i, j: (0, 0)),
            pl.BlockSpec((d, tf), lambda i, j: (0, j)),
            pl.BlockSpec((d, tf), lambda i, j: (0, j)),
            pl.BlockSpec((tf, d), lambda i, j: (j, 0)),
            pl.BlockSpec((1, d), lambda i, j: (0, 0)),
        ] + cast_specs,
        out_specs=[pl.BlockSpec((tm, d), lambda i, j: (i, 0))] + cast_specs,
        out_shape=[jax.ShapeDtypeStruct((t, d), jnp.float32)]
        + [jax.ShapeDtypeStruct(w.shape, jnp.bfloat16) for w in cast],
        scratch_shapes=[pltpu.VMEM((tm, d), jnp.bfloat16)],
        compiler_params=pltpu.CompilerParams(
            dimension_semantics=("arbitrary", "arbitrary"),
            vmem_limit_bytes=VMEM_LIMIT_BYTES),
        name=("ffn_final" if final_norm else "ffn") + ("_cast" if cast else ""),
    )(x, gain, wg, wu, wd, fgain, *cast)
    return outs[0], tuple(outs[1:])


def _rope(x, cos, sin, low_half):
    half = ROT_DIM // 2
    partner = jnp.where(low_half, pltpu.roll(x, HEAD_DIM - half, 1), pltpu.roll(x, half, 1))
    return x * cos + partner * sin


def _inproj_kernel(h_ref, g_ref, w_ref, vg_ref, cos_ref, sin_ref,
                   u_ref, vn_ref, q_ref, k_ref, v_ref, *, gw, qw, kw):
    n = _rms(h_ref[...], g_ref[...]).astype(jnp.bfloat16)
    c1, c2, c3, c4 = gw, 2 * gw, 2 * gw + qw, 2 * gw + qw + kw

    def proj(lo, hi):
        return jnp.dot(n, w_ref[:, lo:hi], preferred_element_type=jnp.float32)

    u_ref[...] = jax.nn.gelu(proj(0, c1)).astype(jnp.bfloat16)
    vn_ref[...] = _rms(jax.nn.gelu(proj(c1, c2)), vg_ref[...]).astype(jnp.bfloat16)

    cos = cos_ref[...]
    sin = sin_ref[...]
    low_half = lax.broadcasted_iota(jnp.int32, cos.shape, 1) < (ROT_DIM // 2)
    zq = proj(c2, c3)
    for hd in range(qw // HEAD_DIM):
        sl = slice(hd * HEAD_DIM, (hd + 1) * HEAD_DIM)
        q_ref[:, sl] = _rope(zq[:, sl], cos, sin, low_half).astype(jnp.bfloat16)
    zk = proj(c3, c4)
    for hd in range(kw // HEAD_DIM):
        sl = slice(hd * HEAD_DIM, (hd + 1) * HEAD_DIM)
        k_ref[:, sl] = _rope(zk[:, sl], cos, sin, low_half).astype(jnp.bfloat16)
    v_ref[...] = proj(c4, c4 + kw).astype(jnp.bfloat16)


def _inproj(h, gain, w_in, vgain, cos, sin, *, seq, tm):
    t, d = h.shape
    gw = vgain.shape[1]
    qw = d - gw
    kw = qw // GQA_GROUP
    cols = w_in.shape[1]
    tiles_per_seq = seq // tm
    bf = jnp.bfloat16
    return pl.pallas_call(
        functools.partial(_inproj_kernel, gw=gw, qw=qw, kw=kw),
        grid=(t // tm,),
        in_specs=[
            pl.BlockSpec((tm, d), lambda i: (i, 0)),
            pl.BlockSpec((1, d), lambda i: (0, 0)),
            pl.BlockSpec((d, cols), lambda i: (0, 0), pipeline_mode=pl.Buffered(1)),
            pl.BlockSpec((1, gw), lambda i: (0, 0)),
            pl.BlockSpec((tm, HEAD_DIM), lambda i: (i % tiles_per_seq, 0)),
            pl.BlockSpec((tm, HEAD_DIM), lambda i: (i % tiles_per_seq, 0)),
        ],
        out_specs=[
            pl.BlockSpec((tm, gw), lambda i: (i, 0)),
            pl.BlockSpec((tm, gw), lambda i: (i, 0)),
            pl.BlockSpec((tm, qw), lambda i: (i, 0)),
            pl.BlockSpec((tm, kw), lambda i: (i, 0)),
            pl.BlockSpec((tm, kw), lambda i: (i, 0)),
        ],
        out_shape=[
            jax.ShapeDtypeStruct((t, gw), bf),
            jax.ShapeDtypeStruct((t, gw), bf),
            jax.ShapeDtypeStruct((t, qw), bf),
            jax.ShapeDtypeStruct((t, kw), bf),
            jax.ShapeDtypeStruct((t, kw), bf),
        ],
        compiler_params=pltpu.CompilerParams(
            dimension_semantics=("arbitrary",),
            vmem_limit_bytes=VMEM_LIMIT_BYTES),
        name="inproj",
    )(h, gain, w_in, vgain, cos, sin)


def _mixer_kernel(sink_ref, h_ref, u_ref, vn_ref, q_ref, k_ref, v_ref,
                  kp_ref, vp_ref, kn_ref, vn2_ref, ws_ref, bs_ref,
                  ga_ref, gb_ref, wo_ref, o_ref,
                  a_ref, att_ref, m_ref, mprev_ref, kall_ref, vall_ref,
                  *, ntiles, tiles_per_seq, nblk):
    s = pl.program_id(0)
    gw = u_ref.shape[1]
    n_kv = k_ref.shape[1] // HEAD_DIM
    rows = GQA_GROUP * CHUNK

    @pl.when(s == 0)
    def _():
        m_ref[...] = jnp.zeros_like(m_ref)

    mprev_ref[...] = m_ref[...]

    kall_ref[0:CHUNK] = kp_ref[...]
    kall_ref[CHUNK:(nblk + 1) * CHUNK] = k_ref[...]
    kall_ref[(nblk + 1) * CHUNK:] = kn_ref[...]
    vall_ref[0:CHUNK] = vp_ref[...]
    vall_ref[CHUNK:(nblk + 1) * CHUNK] = v_ref[...]
    vall_ref[(nblk + 1) * CHUNK:] = vn2_ref[...]

    t_in_seq = jnp.minimum(s, ntiles - 1) % tiles_per_seq
    prev_thr = jnp.where(t_in_seq == 0, 2 * CHUNK, 0)
    next_thr = jnp.where(t_in_seq == tiles_per_seq - 1, -2 * CHUNK, 0)
    rel = (lax.broadcasted_iota(jnp.int32, (rows, CHUNK), 1)
           - (lax.broadcasted_iota(jnp.int32, (rows, CHUNK), 0) & (CHUNK - 1)))
    log2e = 1.4426950408889634
    c2 = (HEAD_DIM ** -0.5) * log2e
    pw = o_ref.shape[1] // (nblk * n_kv)
    gpi = GMLP_GROUPS // n_kv
    for b in range(nblk):
        pthr = prev_thr if b == 0 else 0
        nthr = next_thr if b == nblk - 1 else 0
        qs = slice(b * CHUNK, (b + 1) * CHUNK)
        for kh in range(n_kv):
            ks = slice(kh * HEAD_DIM, (kh + 1) * HEAD_DIM)
            for g in range(kh * gpi, (kh + 1) * gpi):
                gs = slice(g * CHUNK, (g + 1) * CHUNK)
                mixed = jnp.dot(ws_ref[g], vn_ref[qs, gs], preferred_element_type=jnp.float32)
                a_ref[qs, gs] = u_ref[qs, gs].astype(jnp.float32) * (mixed + bs_ref[g])
            q = jnp.concatenate(
                [q_ref[qs, (kh * GQA_GROUP + g) * HEAD_DIM:(kh * GQA_GROUP + g + 1) * HEAD_DIM]
                 for g in range(GQA_GROUP)], axis=0)
            kwin = kall_ref[b * CHUNK:(b + 3) * CHUNK, ks]
            vwin = vall_ref[b * CHUNK:(b + 3) * CHUNK, ks]
            sc = lax.dot_general(q, kwin, (((1,), (1,)), ((), ())),
                                 preferred_element_type=jnp.float32)
            ps = slice((b * n_kv + kh) * pw, (b * n_kv + kh + 1) * pw)
            o_ref[:, ps] = h_ref[:, ps] + jnp.dot(mprev_ref[...], wo_ref[:, ps],
                                                  preferred_element_type=jnp.float32)
            s_prev = jnp.where(rel >= pthr, sc[:, :CHUNK], NEG_INF)
            s_mid = sc[:, CHUNK:2 * CHUNK]
            s_next = jnp.where(rel <= nthr, sc[:, 2 * CHUNK:], NEG_INF)
            sink2 = jnp.concatenate(
                [jnp.full((CHUNK, 1), sink_ref[kh * GQA_GROUP + g] * log2e, jnp.float32)
                 for g in range(GQA_GROUP)], axis=0)
            mx = jnp.maximum(
                jnp.max(jnp.maximum(jnp.maximum(s_prev, s_mid), s_next), axis=-1, keepdims=True) * c2,
                sink2)
            p_prev = jnp.exp2(s_prev * c2 - mx)
            p_mid = jnp.exp2(s_mid * c2 - mx)
            p_next = jnp.exp2(s_next * c2 - mx)
            denom = (jnp.sum(p_prev + p_mid + p_next, axis=-1, keepdims=True)
                     + jnp.exp2(sink2 - mx))
            p = jnp.concatenate([p_prev, p_mid, p_next], axis=1).astype(jnp.bfloat16)
            o = jnp.dot(p, vwin, preferred_element_type=jnp.float32) / denom
            for g in range(GQA_GROUP):
                hs = (kh * GQA_GROUP + g) * HEAD_DIM
                att_ref[qs, hs:hs + HEAD_DIM] = o[g * CHUNK:(g + 1) * CHUNK]
        m_ref[qs, :gw] = _rms(a_ref[qs, :], ga_ref[...]).astype(jnp.bfloat16)
        m_ref[qs, gw:] = _rms(att_ref[qs, :], gb_ref[...]).astype(jnp.bfloat16)


def _mixer(h, u, vn, q, k, v, sink, ws, bs, ga, gb, wo, *, seq, tm):
    t, d = h.shape
    gw = u.shape[1]
    qw = q.shape[1]
    kw = k.shape[1]
    nblk = tm // CHUNK
    nblocks = t // CHUNK
    ntiles = t // tm
    tiles_per_seq = seq // tm
    mix_tile = lambda s: jnp.minimum(s, ntiles - 1)
    mix_row = lambda s: (mix_tile(s), 0)
    proj_row = lambda s: (jnp.maximum(s - 1, 0), 0)
    const2 = lambda s: (0, 0)
    const3 = lambda s: (0, 0, 0)
    prev_blk = lambda s: (jnp.maximum(mix_tile(s) * nblk - 1, 0), 0)
    next_blk = lambda s: (jnp.minimum((mix_tile(s) + 1) * nblk, nblocks - 1), 0)
    return pl.pallas_call(
        functools.partial(_mixer_kernel, ntiles=ntiles, tiles_per_seq=tiles_per_seq, nblk=nblk),
        grid=(ntiles + 1,),
        in_specs=[
            pl.BlockSpec(memory_space=pltpu.SMEM),
            pl.BlockSpec((tm, d), proj_row),
            pl.BlockSpec((tm, gw), mix_row),
            pl.BlockSpec((tm, gw), mix_row),
            pl.BlockSpec((tm, qw), mix_row),
            pl.BlockSpec((tm, kw), mix_row),
            pl.BlockSpec((tm, kw), mix_row),
            pl.BlockSpec((CHUNK, kw), prev_blk),
            pl.BlockSpec((CHUNK, kw), prev_blk),
            pl.BlockSpec((CHUNK, kw), next_blk),
            pl.BlockSpec((CHUNK, kw), next_blk),
            pl.BlockSpec((GMLP_GROUPS, CHUNK, CHUNK), const3),
            pl.BlockSpec((GMLP_GROUPS, CHUNK, CHUNK), const3),
            pl.BlockSpec((1, gw), const2),
            pl.BlockSpec((1, qw), const2),
            pl.BlockSpec((d, d), const2, pipeline_mode=pl.Buffered(1)),
        ],
        out_specs=pl.BlockSpec((tm, d), proj_row),
        out_shape=jax.ShapeDtypeStruct((t, d), jnp.float32),
        scratch_shapes=[
            pltpu.VMEM((tm, gw), jnp.float32),
            pltpu.VMEM((tm, qw), jnp.float32),
            pltpu.VMEM((tm, d), jnp.bfloat16),
            pltpu.VMEM((tm, d), jnp.bfloat16),
            pltpu.VMEM((tm + 2 * CHUNK, kw), jnp.bfloat16),
            pltpu.VMEM((tm + 2 * CHUNK, kw), jnp.bfloat16),
        ],
        compiler_params=pltpu.CompilerParams(
            dimension_semantics=("arbitrary",),
            vmem_limit_bytes=VMEM_LIMIT_BYTES),
        name="mixer",
    )(sink, h, u, vn, q, k, v, k, v, k, v, ws, bs, ga, gb, wo)


def _rope_tables(seq):
    half = ROT_DIM // 2
    inv_freq = ROPE_THETA ** (-jnp.arange(0, ROT_DIM, 2, dtype=jnp.float32) / ROT_DIM)
    ang = jnp.arange(seq, dtype=jnp.float32)[:, None] * inv_freq[None, :]
    cos, sin = lax.optimization_barrier((jnp.cos(ang), jnp.sin(ang)))
    pad = HEAD_DIM - ROT_DIM
    cos_t = jnp.concatenate([cos, cos, jnp.ones((seq, pad), jnp.float32)], axis=1)
    sin_t = jnp.concatenate([-sin, sin, jnp.zeros((seq, pad), jnp.float32)], axis=1)
    return cos_t, sin_t


def kernel(x_prompt, x_sample, ffn1_norm, ffn1_w_gate, ffn1_w_up, ffn1_w_down, mix_norm, w_in,
           gmlp_v_norm, gmlp_w_s, gmlp_b_s, attn_sink, out_norm_gmlp, out_norm_attn, w_out,
           ffn2_norm, ffn2_w_gate, ffn2_w_up, ffn2_w_down, final_norm):
    bf = jnp.bfloat16
    assert ffn1_norm.shape[0] == 1, "one encoder layer"
    d = x_prompt.shape[-1]
    row = lambda a: a.reshape(1, -1)
    fg = row(final_norm)
    ws = gmlp_w_s[0].astype(bf)
    bs = jnp.broadcast_to(gmlp_b_s[0][:, :, None], gmlp_b_s[0].shape + (CHUNK,))
    cos, sin = _rope_tables(max(x_prompt.shape[1], x_sample.shape[1]))

    def flat(x):
        return x.reshape(-1, d)

    def rest_of_layer(h, seq, wg2, wu2, wd2, w_in_bf, w_out_bf):
        u, vn, q, k, v = _inproj(h, row(mix_norm[0]), w_in_bf, row(gmlp_v_norm[0]), cos, sin,
                                 seq=seq, tm=INPROJ_TM)
        h = _mixer(h, u, vn, q, k, v, attn_sink[0], ws, bs, row(out_norm_gmlp[0]),
                   row(out_norm_attn[0]), w_out_bf, seq=seq, tm=MIXER_TM)
        h, _ = _ffn(h, row(ffn2_norm[0]), wg2, wu2, wd2, fg, final_norm=True, tm=FFN_TM, tf=FFN_TF)
        return h

    wg1, wu1, wd1 = (w[0].astype(bf) for w in (ffn1_w_gate, ffn1_w_up, ffn1_w_down))
    later = (ffn2_w_gate[0], ffn2_w_up[0], ffn2_w_down[0], w_in[0], w_out[0])
    h_p, later_bf = _ffn(flat(x_prompt), row(ffn1_norm[0]), wg1, wu1, wd1, fg, final_norm=False,
                         tm=FFN_TM, tf=FFN_TF, cast=later)
    h_s, _ = _ffn(flat(x_sample), row(ffn1_norm[0]), wg1, wu1, wd1, fg, final_norm=False,
                  tm=FFN_TM, tf=FFN_TF)
    y_p = rest_of_layer(h_p, x_prompt.shape[1], *later_bf)
    y_s = rest_of_layer(h_s, x_sample.shape[1], *later_bf)
    return (y_p.reshape(x_prompt.shape), y_s.reshape(x_sample.shape))
```

```python
import functools

import jax
import jax.numpy as jnp
from jax import lax
from jax.experimental import pallas as pl
from jax.experimental.pallas import tpu as pltpu

HEAD_DIM = 128
GQA_GROUP = 4
GMLP_GROUPS = 8
CHUNK = 128
ROT_DIM = HEAD_DIM // 4
ROPE_THETA = 500000.0
EPS = 1e-6
NEG_INF = -1e30

VMEM_LIMIT_BYTES = 60 * 1024 * 1024

FFN_TM = 1024
FFN_TF = 512
FFN_SLAB = 256
FFN_ROW_CHUNK = 128
CAST_BLOCK_LONG = 512
INPROJ_TM = 512
PHASE_ROWS = 8
MIXER_TM = 512


def _rms(x, gain):
    return x * lax.rsqrt(jnp.mean(x * x, axis=-1, keepdims=True) + EPS) * gain


def _ffn_kernel(*refs, final_norm, n_cast):
    x_ref, g_ref, wg_ref, wu_ref, wd_ref, fg_ref = refs[:6]
    cast_in = refs[6:6 + n_cast]
    o_ref = refs[6 + n_cast]
    cast_out = refs[7 + n_cast:7 + 2 * n_cast]
    n_ref = refs[7 + 2 * n_cast]
    j = pl.program_id(1)

    n_chunks = x_ref.shape[0] // FFN_ROW_CHUNK

    def rows(r):
        return pl.ds(pl.multiple_of(r * FFN_ROW_CHUNK, FFN_ROW_CHUNK), FFN_ROW_CHUNK)

    @pl.when(j == 0)
    def _():
        def norm_chunk(r, carry):
            x = x_ref[rows(r), :]
            n_ref[rows(r), :] = _rms(x, g_ref[...]).astype(jnp.bfloat16)
            o_ref[rows(r), :] = x
            return carry
        lax.fori_loop(0, n_chunks, norm_chunk, 0)

    for src, dst in zip(cast_in, cast_out):
        dst[...] = src[...].astype(dst.dtype)

    n = n_ref[...]
    tf = wg_ref.shape[1]
    acts = []
    for c in range(0, tf, FFN_SLAB):
        gate = jnp.dot(n, wg_ref[:, c:c + FFN_SLAB], preferred_element_type=jnp.float32)
        up = jnp.dot(n, wu_ref[:, c:c + FFN_SLAB], preferred_element_type=jnp.float32)
        acts.append((0.5 * jax.nn.silu(gate) * up).astype(jnp.bfloat16))
    o_ref[...] += jnp.dot(jnp.concatenate(acts, axis=1), wd_ref[...],
                          preferred_element_type=jnp.float32)

    if final_norm:
        @pl.when(j == pl.num_programs(1) - 1)
        def _():
            def norm_chunk(r, carry):
                o_ref[rows(r), :] = _rms(o_ref[rows(r), :], fg_ref[...])
                return carry
            lax.fori_loop(0, n_chunks, norm_chunk, 0)


def _cast_plan(w, n_row_steps, n_col_steps):
    r, c = w.shape
    transpose = r > c
    assert min(r, c) % n_row_steps == 0 and max(r, c) % CAST_BLOCK_LONG == 0
    assert max(r, c) // CAST_BLOCK_LONG <= n_col_steps
    br = r // n_row_steps if not transpose else CAST_BLOCK_LONG
    bc = CAST_BLOCK_LONG if not transpose else c // n_row_steps
    if transpose:
        nlong = r // br
        return (br, bc), (lambda i, j: (jnp.minimum(j, nlong - 1), i))
    nlong = c // bc
    return (br, bc), (lambda i, j: (i, jnp.minimum(j, nlong - 1)))


def _ffn(x, gain, wg, wu, wd, fgain, *, final_norm, tm, tf, cast=()):
    t, d = x.shape
    dff = wg.shape[1]
    grid = (t // tm, dff // tf)
    plans = [_cast_plan(w, *grid) for w in cast]
    cast_specs = [pl.BlockSpec(blk, imap) for blk, imap in plans]
    outs = pl.pallas_call(
        functools.partial(_ffn_kernel, final_norm=final_norm, n_cast=len(cast)),
        grid=grid,
        in_specs=[
            pl.BlockSpec((tm, d), lambda i, j: (i, 0)),
            pl.BlockSpec((1, d), lambda i, j: (0, 0)),
            pl.BlockSpec((d, tf), lambda i, j: (0, j)),
            pl.BlockSpec((d, tf), lambda i, j: (0, j)),
            pl.BlockSpec((tf, d), lambda i, j: (j, 0)),
            pl.BlockSpec((1, d), lambda i, j: (0, 0)),
        ] + cast_specs,
        out_specs=[pl.BlockSpec((tm, d), lambda i, j: (i, 0))] + cast_specs,
        out_shape=[jax.ShapeDtypeStruct((t, d), jnp.float32)]
        + [jax.ShapeDtypeStruct(w.shape, jnp.bfloat16) for w in cast],
        scratch_shapes=[pltpu.VMEM((tm, d), jnp.bfloat16)],
        compiler_params=pltpu.CompilerParams(
            dimension_semantics=("arbitrary", "arbitrary"),
            vmem_limit_bytes=VMEM_LIMIT_BYTES),
        name=("ffn_final" if final_norm else "ffn") + ("_cast" if cast else ""),
    )(x, gain, wg, wu, wd, fgain, *cast)
    return outs[0], tuple(outs[1:])


def _rope(x, cos, sin, low_half):
    half = ROT_DIM // 2
    partner = jnp.where(low_half, pltpu.roll(x, HEAD_DIM - half, 1), pltpu.roll(x, half, 1))
    return x * cos + partner * sin


def _inproj_kernel(h_ref, g_ref, w_ref, vg_ref, cr_ref, sr_ref, ph_ref,
                   u_ref, vn_ref, q_ref, k_ref, v_ref, *, gw, qw, kw):
    n = _rms(h_ref[...], g_ref[...]).astype(jnp.bfloat16)
    c1, c2, c3, c4 = gw, 2 * gw, 2 * gw + qw, 2 * gw + qw + kw

    def proj(lo, hi):
        return jnp.dot(n, w_ref[:, lo:hi], preferred_element_type=jnp.float32)

    u_ref[...] = jax.nn.gelu(proj(0, c1)).astype(jnp.bfloat16)
    vn_ref[...] = _rms(jax.nn.gelu(proj(c1, c2)), vg_ref[...]).astype(jnp.bfloat16)

    cr = cr_ref[...]
    sr = sr_ref[...]
    cos = ph_ref[0:1, :] * cr - ph_ref[1:2, :] * sr
    sin = ph_ref[3:4, :] * cr + ph_ref[2:3, :] * sr
    low_half = lax.broadcasted_iota(jnp.int32, cos.shape, 1) < (ROT_DIM // 2)

    zk = proj(c3, c4)
    for hd in range(kw // HEAD_DIM):
        sl = slice(hd * HEAD_DIM, (hd + 1) * HEAD_DIM)
        k_ref[:, sl] = _rope(zk[:, sl], cos, sin, low_half).astype(jnp.bfloat16)
    zq = proj(c2, c3)
    for hd in range(qw // HEAD_DIM):
        sl = slice(hd * HEAD_DIM, (hd + 1) * HEAD_DIM)
        q_ref[:, sl] = _rope(zq[:, sl], cos, sin, low_half).astype(jnp.bfloat16)
    v_ref[...] = proj(c4, c4 + kw).astype(jnp.bfloat16)


def _inproj(h, gain, w_in, vgain, rope, *, seq, tm):
    t, d = h.shape
    gw = vgain.shape[1]
    qw = d - gw
    kw = qw // GQA_GROUP
    cols = w_in.shape[1]
    tiles_per_seq = seq // tm
    cr, sr, phase = rope
    bf = jnp.bfloat16
    row = lambda i: (i, 0)
    const = lambda i: (0, 0)
    return pl.pallas_call(
        functools.partial(_inproj_kernel, gw=gw, qw=qw, kw=kw),
        grid=(t // tm,),
        in_specs=[
            pl.BlockSpec((tm, d), row),
            pl.BlockSpec((1, d), const),
            pl.BlockSpec((d, cols), const, pipeline_mode=pl.Buffered(1)),
            pl.BlockSpec((1, gw), const),
            pl.BlockSpec((tm, HEAD_DIM), const),
            pl.BlockSpec((tm, HEAD_DIM), const),
            pl.BlockSpec((PHASE_ROWS, HEAD_DIM), lambda i: (i % tiles_per_seq, 0)),
        ],
        out_specs=[
            pl.BlockSpec((tm, gw), row),
            pl.BlockSpec((tm, gw), row),
            pl.BlockSpec((tm, qw), row),
            pl.BlockSpec((tm, kw), row),
            pl.BlockSpec((tm, kw), row),
        ],
        out_shape=[
            jax.ShapeDtypeStruct((t, gw), bf),
            jax.ShapeDtypeStruct((t, gw), bf),
            jax.ShapeDtypeStruct((t, qw), bf),
            jax.ShapeDtypeStruct((t, kw), bf),
            jax.ShapeDtypeStruct((t, kw), bf),
        ],
        compiler_params=pltpu.CompilerParams(
            dimension_semantics=("arbitrary",),
            vmem_limit_bytes=VMEM_LIMIT_BYTES),
        name="inproj",
    )(h, gain, w_in, vgain, cr, sr, phase)


def _mixer_kernel(sink_ref, h_ref, u_ref, vn_ref, q_ref, k_ref, v_ref,
                  kp_ref, vp_ref, kn_ref, vn2_ref, ws_ref, bs_ref,
                  ga_ref, gb_ref, wo_ref, o_ref,
                  a_ref, att_ref, m_ref, mprev_ref, kall_ref, vall_ref,
                  *, ntiles, tiles_per_seq, nblk):
    s = pl.program_id(0)
    gw = u_ref.shape[1]
    n_kv = k_ref.shape[1] // HEAD_DIM
    rows = GQA_GROUP * CHUNK

    @pl.when(s == 0)
    def _():
        m_ref[...] = jnp.zeros_like(m_ref)

    mprev_ref[...] = m_ref[...]

    kall_ref[0:CHUNK] = kp_ref[...]
    kall_ref[CHUNK:(nblk + 1) * CHUNK] = k_ref[...]
    kall_ref[(nblk + 1) * CHUNK:] = kn_ref[...]
    vall_ref[0:CHUNK] = vp_ref[...]
    vall_ref[CHUNK:(nblk + 1) * CHUNK] = v_ref[...]
    vall_ref[(nblk + 1) * CHUNK:] = vn2_ref[...]

    t_in_seq = jnp.minimum(s, ntiles - 1) % tiles_per_seq
    prev_thr = jnp.where(t_in_seq == 0, 2 * CHUNK, 0)
    next_thr = jnp.where(t_in_seq == tiles_per_seq - 1, -2 * CHUNK, 0)
    rel = (lax.broadcasted_iota(jnp.int32, (rows, CHUNK), 1)
           - (lax.broadcasted_iota(jnp.int32, (rows, CHUNK), 0) & (CHUNK - 1)))
    log2e = 1.4426950408889634
    c2 = (HEAD_DIM ** -0.5) * log2e
    pw = o_ref.shape[1] // (nblk * n_kv)
    gpi = GMLP_GROUPS // n_kv
    for b in range(nblk):
        pthr = prev_thr if b == 0 else 0
        nthr = next_thr if b == nblk - 1 else 0
        qs = slice(b * CHUNK, (b + 1) * CHUNK)
        for kh in range(n_kv):
            ks = slice(kh * HEAD_DIM, (kh + 1) * HEAD_DIM)
            for g in range(kh * gpi, (kh + 1) * gpi):
                gs = slice(g * CHUNK, (g + 1) * CHUNK)
                mixed = jnp.dot(ws_ref[g], vn_ref[qs, gs], preferred_element_type=jnp.float32)
                a_ref[qs, gs] = u_ref[qs, gs].astype(jnp.float32) * (mixed + bs_ref[g])
            q = jnp.concatenate(
                [q_ref[qs, (kh * GQA_GROUP + g) * HEAD_DIM:(kh * GQA_GROUP + g + 1) * HEAD_DIM]
                 for g in range(GQA_GROUP)], axis=0)
            kwin = kall_ref[b * CHUNK:(b + 3) * CHUNK, ks]
            vwin = vall_ref[b * CHUNK:(b + 3) * CHUNK, ks]
            sc = lax.dot_general(q, kwin, (((1,), (1,)), ((), ())),
                                 preferred_element_type=jnp.float32)
            ps = slice((b * n_kv + kh) * pw, (b * n_kv + kh + 1) * pw)
            o_ref[:, ps] = h_ref[:, ps] + jnp.dot(mprev_ref[...], wo_ref[:, ps],
                                                  preferred_element_type=jnp.float32)
            s_prev = jnp.where(rel >= pthr, sc[:, :CHUNK], NEG_INF)
            s_mid = sc[:, CHUNK:2 * CHUNK]
            s_next = jnp.where(rel <= nthr, sc[:, 2 * CHUNK:], NEG_INF)
            sink2 = jnp.concatenate(
                [jnp.full((CHUNK, 1), sink_ref[kh * GQA_GROUP + g] * log2e, jnp.float32)
                 for g in range(GQA_GROUP)], axis=0)
            mx = jnp.maximum(
                jnp.max(jnp.maximum(jnp.maximum(s_prev, s_mid), s_next), axis=-1, keepdims=True) * c2,
                sink2)
            p_prev = jnp.exp2(s_prev * c2 - mx)
            p_mid = jnp.exp2(s_mid * c2 - mx)
            p_next = jnp.exp2(s_next * c2 - mx)
            denom = (jnp.sum(p_prev + p_mid + p_next, axis=-1, keepdims=True)
                     + jnp.exp2(sink2 - mx))
            p = jnp.concatenate([p_prev, p_mid, p_next], axis=1).astype(jnp.bfloat16)
            o = jnp.dot(p, vwin, preferred_element_type=jnp.float32) / denom
            for g in range(GQA_GROUP):
                hs = (kh * GQA_GROUP + g) * HEAD_DIM
                att_ref[qs, hs:hs + HEAD_DIM] = o[g * CHUNK:(g + 1) * CHUNK]
        m_ref[qs, :gw] = _rms(a_ref[qs, :], ga_ref[...]).astype(jnp.bfloat16)
        m_ref[qs, gw:] = _rms(att_ref[qs, :], gb_ref[...]).astype(jnp.bfloat16)


def _mixer(h, u, vn, q, k, v, sink, ws, bs, ga, gb, wo, *, seq, tm):
    t, d = h.shape
    gw = u.shape[1]
    qw = q.shape[1]
    kw = k.shape[1]
    nblk = tm // CHUNK
    nblocks = t // CHUNK
    ntiles = t // tm
    tiles_per_seq = seq // tm
    mix_tile = lambda s: jnp.minimum(s, ntiles - 1)
    mix_row = lambda s: (mix_tile(s), 0)
    proj_row = lambda s: (jnp.maximum(s - 1, 0), 0)
    const2 = lambda s: (0, 0)
    const3 = lambda s: (0, 0, 0)
    prev_blk = lambda s: (jnp.maximum(mix_tile(s) * nblk - 1, 0), 0)
    next_blk = lambda s: (jnp.minimum((mix_tile(s) + 1) * nblk, nblocks - 1), 0)
    return pl.pallas_call(
        functools.partial(_mixer_kernel, ntiles=ntiles, tiles_per_seq=tiles_per_seq, nblk=nblk),
        grid=(ntiles + 1,),
        in_specs=[
            pl.BlockSpec(memory_space=pltpu.SMEM),
            pl.BlockSpec((tm, d), proj_row),
            pl.BlockSpec((tm, gw), mix_row),
            pl.BlockSpec((tm, gw), mix_row),
            pl.BlockSpec((tm, qw), mix_row),
            pl.BlockSpec((tm, kw), mix_row),
            pl.BlockSpec((tm, kw), mix_row),
            pl.BlockSpec((CHUNK, kw), prev_blk),
            pl.BlockSpec((CHUNK, kw), prev_blk),
            pl.BlockSpec((CHUNK, kw), next_blk),
            pl.BlockSpec((CHUNK, kw), next_blk),
            pl.BlockSpec((GMLP_GROUPS, CHUNK, CHUNK), const3),
            pl.BlockSpec((GMLP_GROUPS, CHUNK, CHUNK), const3),
            pl.BlockSpec((1, gw), const2),
            pl.BlockSpec((1, qw), const2),
            pl.BlockSpec((d, d), const2, pipeline_mode=pl.Buffered(1)),
        ],
        out_specs=pl.BlockSpec((tm, d), proj_row),
        out_shape=jax.ShapeDtypeStruct((t, d), jnp.float32),
        scratch_shapes=[
            pltpu.VMEM((tm, gw), jnp.float32),
            pltpu.VMEM((tm, qw), jnp.float32),
            pltpu.VMEM((tm, d), jnp.bfloat16),
            pltpu.VMEM((tm, d), jnp.bfloat16),
            pltpu.VMEM((tm + 2 * CHUNK, kw), jnp.bfloat16),
            pltpu.VMEM((tm + 2 * CHUNK, kw), jnp.bfloat16),
        ],
        compiler_params=pltpu.CompilerParams(
            dimension_semantics=("arbitrary",),
            vmem_limit_bytes=VMEM_LIMIT_BYTES),
        name="mixer",
    )(sink, h, u, vn, q, k, v, k, v, k, v, ws, bs, ga, gb, wo)


def _rope_tables(max_seq, tm):
    half = ROT_DIM // 2
    inv_freq = ROPE_THETA ** (-jnp.arange(0, ROT_DIM, 2, dtype=jnp.float32) / ROT_DIM)
    pad = jnp.zeros((HEAD_DIM - ROT_DIM,), jnp.float32)
    freq = jnp.concatenate([inv_freq, inv_freq, pad])[None, :]
    sign = jnp.concatenate([-jnp.ones((half,)), jnp.ones((half,)), pad])[None, :]
    ang_r = jnp.arange(tm, dtype=jnp.float32)[:, None] * freq
    ang_0 = jnp.arange(0, max_seq, tm, dtype=jnp.float32)[:, None] * freq
    c0, s0 = jnp.cos(ang_0), jnp.sin(ang_0)
    zero = jnp.zeros_like(c0)
    phase = jnp.stack([c0, s0, sign * c0, sign * s0] + [zero] * (PHASE_ROWS - 4), axis=1)
    return jnp.cos(ang_r), jnp.sin(ang_r), phase.reshape(-1, HEAD_DIM)


def kernel(x_prompt, x_sample, ffn1_norm, ffn1_w_gate, ffn1_w_up, ffn1_w_down, mix_norm, w_in,
           gmlp_v_norm, gmlp_w_s, gmlp_b_s, attn_sink, out_norm_gmlp, out_norm_attn, w_out,
           ffn2_norm, ffn2_w_gate, ffn2_w_up, ffn2_w_down, final_norm):
    bf = jnp.bfloat16
    assert ffn1_norm.shape[0] == 1, "one encoder layer"
    d = x_prompt.shape[-1]
    row = lambda a: a.reshape(1, -1)
    fg = row(final_norm)
    ws = gmlp_w_s[0].astype(bf)
    bs = jnp.broadcast_to(gmlp_b_s[0][:, :, None], gmlp_b_s[0].shape + (CHUNK,))
    rope = _rope_tables(max(x_prompt.shape[1], x_sample.shape[1]), INPROJ_TM)

    def flat(x):
        return x.reshape(-1, d)

    def rest_of_layer(h, seq, wg2, wu2, wd2, w_in_bf, w_out_bf):
        u, vn, q, k, v = _inproj(h, row(mix_norm[0]), w_in_bf, row(gmlp_v_norm[0]), rope,
                                 seq=seq, tm=INPROJ_TM)
        h = _mixer(h, u, vn, q, k, v, attn_sink[0], ws, bs, row(out_norm_gmlp[0]),
                   row(out_norm_attn[0]), w_out_bf, seq=seq, tm=MIXER_TM)
        h, _ = _ffn(h, row(ffn2_norm[0]), wg2, wu2, wd2, fg, final_norm=True, tm=FFN_TM, tf=FFN_TF)
        return h

    wg1, wu1, wd1 = (w[0].astype(bf) for w in (ffn1_w_gate, ffn1_w_up, ffn1_w_down))
    later = (ffn2_w_gate[0], ffn2_w_up[0], ffn2_w_down[0], w_in[0], w_out[0])
    h_p, later_bf = _ffn(flat(x_prompt), row(ffn1_norm[0]), wg1, wu1, wd1, fg, final_norm=False,
                         tm=FFN_TM, tf=FFN_TF, cast=later)
    h_s, _ = _ffn(flat(x_sample), row(ffn1_norm[0]), wg1, wu1, wd1, fg, final_norm=False,
                  tm=FFN_TM, tf=FFN_TF)
    y_p = rest_of_layer(h_p, x_prompt.shape[1], *later_bf)
    y_s = rest_of_layer(h_s, x_sample.shape[1], *later_bf)
    return (y_p.reshape(x_prompt.shape), y_s.reshape(x_sample.shape))
```

```python
import functools

import jax
import jax.numpy as jnp
from jax import lax
from jax.experimental import pallas as pl
from jax.experimental.pallas import tpu as pltpu

HEAD_DIM = 128
GQA_GROUP = 4
GMLP_GROUPS = 8
CHUNK = 128
ROT_DIM = HEAD_DIM // 4
ROPE_THETA = 500000.0
EPS = 1e-6
NEG_INF = -1e30

VMEM_LIMIT_BYTES = 60 * 1024 * 1024

FFN_TM = 1024
FFN_TF = 512
FFN_SLAB = 256
FFN_ROW_CHUNK = 128
CAST_BLOCK_LONG = 512
INPROJ_TM = 512
PHASE_ROWS = 8
MIXER_TM = 512
MIXER_TAIL_SLABS = 2


def _rms(x, gain):
    return x * lax.rsqrt(jnp.mean(x * x, axis=-1, keepdims=True) + EPS) * gain


def _ffn_kernel(*refs, final_norm, n_cast):
    x0_ref, xn_ref, xr_ref, g_ref, wg_ref, wu_ref, wd_ref, fg_ref = refs[:8]
    cast_in = refs[8:8 + n_cast]
    o_ref = refs[8 + n_cast]
    cast_out = refs[9 + n_cast:9 + 2 * n_cast]
    ncur_ref, nnext_ref = refs[9 + 2 * n_cast:]
    i = pl.program_id(0)
    j = pl.program_id(1)
    n_steps = pl.num_programs(1)
    n_chunks = o_ref.shape[0] // FFN_ROW_CHUNK

    def rows(r):
        return pl.ds(pl.multiple_of(r * FFN_ROW_CHUNK, FFN_ROW_CHUNK), FFN_ROW_CHUNK)

    @pl.when((i == 0) & (j == 0))
    def _():
        def norm_chunk(r, carry):
            nnext_ref[rows(r), :] = _rms(x0_ref[rows(r), :], g_ref[...]).astype(jnp.bfloat16)
            return carry
        lax.fori_loop(0, n_chunks, norm_chunk, 0)

    def cast_blocks():
        for src, dst in zip(cast_in, cast_out):
            dst[...] = src[...].astype(dst.dtype)

    def swiglu_down(n):
        acts = []
        for c in range(0, wg_ref.shape[1], FFN_SLAB):
            gate = jnp.dot(n, wg_ref[:, c:c + FFN_SLAB], preferred_element_type=jnp.float32)
            up = jnp.dot(n, wu_ref[:, c:c + FFN_SLAB], preferred_element_type=jnp.float32)
            acts.append((0.5 * jax.nn.silu(gate) * up).astype(jnp.bfloat16))
        return jnp.dot(jnp.concatenate(acts, axis=1), wd_ref[...],
                       preferred_element_type=jnp.float32)

    @pl.when(j == 0)
    def _():
        cast_blocks()
        o_ref[...] = swiglu_down(nnext_ref[...])
        ncur_ref[...] = nnext_ref[...]

    @pl.when(j > 0)
    def _():
        norm_chunk = jnp.clip(j - 1, 0, n_chunks - 1)
        nnext_ref[rows(norm_chunk), :] = _rms(xn_ref[...], g_ref[...]).astype(jnp.bfloat16)
        cast_blocks()
        first_res_step = n_steps - n_chunks
        res_chunk = jnp.clip(j - first_res_step, 0, n_chunks - 1)
        res_on = jnp.where(j >= first_res_step, 1.0, 0.0)
        o_ref[rows(res_chunk), :] += xr_ref[...] * res_on
        o_ref[...] += swiglu_down(ncur_ref[...])

    if final_norm:
        @pl.when(j == n_steps - 1)
        def _():
            def norm_chunk(r, carry):
                o_ref[rows(r), :] = _rms(o_ref[rows(r), :], fg_ref[...])
                return carry
            lax.fori_loop(0, n_chunks, norm_chunk, 0)


def _cast_plan(w, n_row_steps, n_col_steps):
    r, c = w.shape
    transpose = r > c
    assert min(r, c) % n_row_steps == 0 and max(r, c) % CAST_BLOCK_LONG == 0
    assert max(r, c) // CAST_BLOCK_LONG <= n_col_steps
    br = r // n_row_steps if not transpose else CAST_BLOCK_LONG
    bc = CAST_BLOCK_LONG if not transpose else c // n_row_steps
    if transpose:
        nlong = r // br
        return (br, bc), (lambda i, j: (jnp.minimum(j, nlong - 1), i))
    nlong = c // bc
    return (br, bc), (lambda i, j: (i, jnp.minimum(j, nlong - 1)))


def _ffn(x, gain, wg, wu, wd, fgain, *, final_norm, tm, tf, cast=()):
    t, d = x.shape
    dff = wg.shape[1]
    n_tiles, n_steps = t // tm, dff // tf
    n_chunks = tm // FFN_ROW_CHUNK
    assert n_steps > n_chunks, "one norm chunk per step after the first, one residual chunk per late step"
    plans = [_cast_plan(w, n_tiles, n_steps) for w in cast]
    cast_specs = [pl.BlockSpec(blk, imap) for blk, imap in plans]

    def next_tile_chunk(i, j):
        tile = jnp.minimum(i + 1, n_tiles - 1)
        return (tile * n_chunks + jnp.clip(j - 1, 0, n_chunks - 1), 0)

    def residual_chunk(i, j):
        return (i * n_chunks + jnp.clip(j - (n_steps - n_chunks), 0, n_chunks - 1), 0)

    outs = pl.pallas_call(
        functools.partial(_ffn_kernel, final_norm=final_norm, n_cast=len(cast)),
        grid=(n_tiles, n_steps),
        in_specs=[
            pl.BlockSpec((tm, d), lambda i, j: (0, 0), pipeline_mode=pl.Buffered(1)),
            pl.BlockSpec((FFN_ROW_CHUNK, d), next_tile_chunk),
            pl.BlockSpec((FFN_ROW_CHUNK, d), residual_chunk),
            pl.BlockSpec((1, d), lambda i, j: (0, 0)),
            pl.BlockSpec((d, tf), lambda i, j: (0, j)),
            pl.BlockSpec((d, tf), lambda i, j: (0, j)),
            pl.BlockSpec((tf, d), lambda i, j: (j, 0)),
            pl.BlockSpec((1, d), lambda i, j: (0, 0)),
        ] + cast_specs,
        out_specs=[pl.BlockSpec((tm, d), lambda i, j: (i, 0))] + cast_specs,
        out_shape=[jax.ShapeDtypeStruct((t, d), jnp.float32)]
        + [jax.ShapeDtypeStruct(w.shape, jnp.bfloat16) for w in cast],
        scratch_shapes=[
            pltpu.VMEM((tm, d), jnp.bfloat16),
            pltpu.VMEM((tm, d), jnp.bfloat16),
        ],
        compiler_params=pltpu.CompilerParams(
            dimension_semantics=("arbitrary", "arbitrary"),
            vmem_limit_bytes=VMEM_LIMIT_BYTES),
        name=("ffn_final" if final_norm else "ffn") + ("_cast" if cast else ""),
    )(x, x, x, gain, wg, wu, wd, fgain, *cast)
    return outs[0], tuple(outs[1:])


def _rope(x, cos, sin, low_half):
    half = ROT_DIM // 2
    partner = jnp.where(low_half, pltpu.roll(x, HEAD_DIM - half, 1), pltpu.roll(x, half, 1))
    return x * cos + partner * sin


def _inproj_kernel(h_ref, g_ref, w_ref, vg_ref, cr_ref, sr_ref, ph_ref,
                   u_ref, vn_ref, q_ref, k_ref, v_ref, *, gw, qw, kw):
    n = _rms(h_ref[...], g_ref[...]).astype(jnp.bfloat16)
    c1, c2, c3, c4 = gw, 2 * gw, 2 * gw + qw, 2 * gw + qw + kw

    def proj(lo, hi):
        return jnp.dot(n, w_ref[:, lo:hi], preferred_element_type=jnp.float32)

    u_ref[...] = jax.nn.gelu(proj(0, c1)).astype(jnp.bfloat16)
    vn_ref[...] = _rms(jax.nn.gelu(proj(c1, c2)), vg_ref[...]).astype(jnp.bfloat16)

    cr = cr_ref[...]
    sr = sr_ref[...]
    cos = ph_ref[0:1, :] * cr - ph_ref[1:2, :] * sr
    sin = ph_ref[3:4, :] * cr + ph_ref[2:3, :] * sr
    low_half = lax.broadcasted_iota(jnp.int32, cos.shape, 1) < (ROT_DIM // 2)

    zk = proj(c3, c4)
    for hd in range(kw // HEAD_DIM):
        sl = slice(hd * HEAD_DIM, (hd + 1) * HEAD_DIM)
        k_ref[:, sl] = _rope(zk[:, sl], cos, sin, low_half).astype(jnp.bfloat16)
    zq = proj(c2, c3)
    for hd in range(qw // HEAD_DIM):
        sl = slice(hd * HEAD_DIM, (hd + 1) * HEAD_DIM)
        q_ref[:, sl] = _rope(zq[:, sl], cos, sin, low_half).astype(jnp.bfloat16)
    v_ref[...] = proj(c4, c4 + kw).astype(jnp.bfloat16)


def _inproj(h, gain, w_in, vgain, rope, *, seq, tm):
    t, d = h.shape
    gw = vgain.shape[1]
    qw = d - gw
    kw = qw // GQA_GROUP
    cols = w_in.shape[1]
    tiles_per_seq = seq // tm
    cr, sr, phase = rope
    bf = jnp.bfloat16
    row = lambda i: (i, 0)
    const = lambda i: (0, 0)
    return pl.pallas_call(
        functools.partial(_inproj_kernel, gw=gw, qw=qw, kw=kw),
        grid=(t // tm,),
        in_specs=[
            pl.BlockSpec((tm, d), row),
            pl.BlockSpec((1, d), const),
            pl.BlockSpec((d, cols), const, pipeline_mode=pl.Buffered(1)),
            pl.BlockSpec((1, gw), const),
            pl.BlockSpec((tm, HEAD_DIM), const),
            pl.BlockSpec((tm, HEAD_DIM), const),
            pl.BlockSpec((PHASE_ROWS, HEAD_DIM), lambda i: (i % tiles_per_seq, 0)),
        ],
        out_specs=[
            pl.BlockSpec((tm, gw), row),
            pl.BlockSpec((tm, gw), row),
            pl.BlockSpec((tm, qw), row),
            pl.BlockSpec((tm, kw), row),
            pl.BlockSpec((tm, kw), row),
        ],
        out_shape=[
            jax.ShapeDtypeStruct((t, gw), bf),
            jax.ShapeDtypeStruct((t, gw), bf),
            jax.ShapeDtypeStruct((t, qw), bf),
            jax.ShapeDtypeStruct((t, kw), bf),
            jax.ShapeDtypeStruct((t, kw), bf),
        ],
        compiler_params=pltpu.CompilerParams(
            dimension_semantics=("arbitrary",),
            vmem_limit_bytes=VMEM_LIMIT_BYTES),
        name="inproj",
    )(h, gain, w_in, vgain, cr, sr, phase)


def _mixer_kernel(sink_ref, h_ref, u_ref, vn_ref, q_ref, k_ref, v_ref,
                  kp_ref, vp_ref, kn_ref, vn2_ref, ws_ref, bs_ref,
                  ga_ref, gb_ref, wo_ref, o_ref,
                  a_ref, att_ref, m_ref, mprev_ref, kall_ref, vall_ref,
                  *, ntiles, tiles_per_seq, nblk):
    s = pl.program_id(0)
    gw = u_ref.shape[1]
    n_kv = k_ref.shape[1] // HEAD_DIM
    rows = GQA_GROUP * CHUNK

    @pl.when(s == 0)
    def _():
        m_ref[...] = jnp.zeros_like(m_ref)

    mprev_ref[...] = m_ref[...]

    kall_ref[0:CHUNK] = kp_ref[...]
    kall_ref[CHUNK:(nblk + 1) * CHUNK] = k_ref[...]
    kall_ref[(nblk + 1) * CHUNK:] = kn_ref[...]
    vall_ref[0:CHUNK] = vp_ref[...]
    vall_ref[CHUNK:(nblk + 1) * CHUNK] = v_ref[...]
    vall_ref[(nblk + 1) * CHUNK:] = vn2_ref[...]

    t_in_seq = jnp.minimum(s, ntiles - 1) % tiles_per_seq
    prev_thr = jnp.where(t_in_seq == 0, 2 * CHUNK, 0)
    next_thr = jnp.where(t_in_seq == tiles_per_seq - 1, -2 * CHUNK, 0)
    rel = (lax.broadcasted_iota(jnp.int32, (rows, CHUNK), 1)
           - (lax.broadcasted_iota(jnp.int32, (rows, CHUNK), 0) & (CHUNK - 1)))
    log2e = 1.4426950408889634
    c2 = (HEAD_DIM ** -0.5) * log2e
    n_slabs = nblk * n_kv
    pw = o_ref.shape[1] // n_slabs
    gpi = GMLP_GROUPS // n_kv

    def project_slab(slab):
        ps = slice(slab * pw, (slab + 1) * pw)
        o_ref[:, ps] = h_ref[:, ps] + jnp.dot(mprev_ref[...], wo_ref[:, ps],
                                              preferred_element_type=jnp.float32)

    def scores(b, kh):
        qs = slice(b * CHUNK, (b + 1) * CHUNK)
        q = jnp.concatenate(
            [q_ref[qs, (kh * GQA_GROUP + g) * HEAD_DIM:(kh * GQA_GROUP + g + 1) * HEAD_DIM]
             for g in range(GQA_GROUP)], axis=0)
        kwin = kall_ref[b * CHUNK:(b + 3) * CHUNK, kh * HEAD_DIM:(kh + 1) * HEAD_DIM]
        return lax.dot_general(q, kwin, (((1,), (1,)), ((), ())),
                               preferred_element_type=jnp.float32)

    iters = [(b, kh) for b in range(nblk) for kh in range(n_kv)]
    sc_next = scores(*iters[0])
    for it, (b, kh) in enumerate(iters):
        pthr = prev_thr if b == 0 else 0
        nthr = next_thr if b == nblk - 1 else 0
        qs = slice(b * CHUNK, (b + 1) * CHUNK)
        sc = sc_next
        if it + 1 < len(iters):
            sc_next = scores(*iters[it + 1])
        if it < n_slabs - MIXER_TAIL_SLABS:
            project_slab(it)
        for g in range(kh * gpi, (kh + 1) * gpi):
            gs = slice(g * CHUNK, (g + 1) * CHUNK)
            mixed = jnp.dot(ws_ref[g], vn_ref[qs, gs], preferred_element_type=jnp.float32)
            a_ref[qs, gs] = u_ref[qs, gs].astype(jnp.float32) * (mixed + bs_ref[g])
        s_prev = jnp.where(rel >= pthr, sc[:, :CHUNK], NEG_INF)
        s_mid = sc[:, CHUNK:2 * CHUNK]
        s_next = jnp.where(rel <= nthr, sc[:, 2 * CHUNK:], NEG_INF)
        sink2 = jnp.concatenate(
            [jnp.full((CHUNK, 1), sink_ref[kh * GQA_GROUP + g] * log2e, jnp.float32)
             for g in range(GQA_GROUP)], axis=0)
        mx = jnp.maximum(
            jnp.max(jnp.maximum(jnp.maximum(s_prev, s_mid), s_next), axis=-1, keepdims=True) * c2,
            sink2)
        p_prev = jnp.exp2(s_prev * c2 - mx)
        p_mid = jnp.exp2(s_mid * c2 - mx)
        p_next = jnp.exp2(s_next * c2 - mx)
        denom = (jnp.sum(p_prev + p_mid + p_next, axis=-1, keepdims=True)
                 + jnp.exp2(sink2 - mx))
        p = jnp.concatenate([p_prev, p_mid, p_next], axis=1).astype(jnp.bfloat16)
        vwin = vall_ref[b * CHUNK:(b + 3) * CHUNK, kh * HEAD_DIM:(kh + 1) * HEAD_DIM]
        o = jnp.dot(p, vwin, preferred_element_type=jnp.float32) / denom
        for g in range(GQA_GROUP):
            hs = (kh * GQA_GROUP + g) * HEAD_DIM
            att_ref[qs, hs:hs + HEAD_DIM] = o[g * CHUNK:(g + 1) * CHUNK]
        if kh == n_kv - 1:
            m_ref[qs, :gw] = _rms(a_ref[qs, :], ga_ref[...]).astype(jnp.bfloat16)
            m_ref[qs, gw:] = _rms(att_ref[qs, :], gb_ref[...]).astype(jnp.bfloat16)
    for slab in range(n_slabs - MIXER_TAIL_SLABS, n_slabs):
        project_slab(slab)


def _mixer(h, u, vn, q, k, v, sink, ws, bs, ga, gb, wo, *, seq, tm):
    t, d = h.shape
    gw = u.shape[1]
    qw = q.shape[1]
    kw = k.shape[1]
    nblk = tm // CHUNK
    nblocks = t // CHUNK
    ntiles = t // tm
    tiles_per_seq = seq // tm
    mix_tile = lambda s: jnp.minimum(s, ntiles - 1)
    mix_row = lambda s: (mix_tile(s), 0)
    proj_row = lambda s: (jnp.maximum(s - 1, 0), 0)
    const2 = lambda s: (0, 0)
    const3 = lambda s: (0, 0, 0)
    prev_blk = lambda s: (jnp.maximum(mix_tile(s) * nblk - 1, 0), 0)
    next_blk = lambda s: (jnp.minimum((mix_tile(s) + 1) * nblk, nblocks - 1), 0)
    return pl.pallas_call(
        functools.partial(_mixer_kernel, ntiles=ntiles, tiles_per_seq=tiles_per_seq, nblk=nblk),
        grid=(ntiles + 1,),
        in_specs=[
            pl.BlockSpec(memory_space=pltpu.SMEM),
            pl.BlockSpec((tm, d), proj_row),
            pl.BlockSpec((tm, gw), mix_row),
            pl.BlockSpec((tm, gw), mix_row),
            pl.BlockSpec((tm, qw), mix_row),
            pl.BlockSpec((tm, kw), mix_row),
            pl.BlockSpec((tm, kw), mix_row),
            pl.BlockSpec((CHUNK, kw), prev_blk),
            pl.BlockSpec((CHUNK, kw), prev_blk),
            pl.BlockSpec((CHUNK, kw), next_blk),
            pl.BlockSpec((CHUNK, kw), next_blk),
            pl.BlockSpec((GMLP_GROUPS, CHUNK, CHUNK), const3),
            pl.BlockSpec((GMLP_GROUPS, CHUNK, CHUNK), const3),
            pl.BlockSpec((1, gw), const2),
            pl.BlockSpec((1, qw), const2),
            pl.BlockSpec((d, d), const2, pipeline_mode=pl.Buffered(1)),
        ],
        out_specs=pl.BlockSpec((tm, d), proj_row),
        out_shape=jax.ShapeDtypeStruct((t, d), jnp.float32),
        scratch_shapes=[
            pltpu.VMEM((tm, gw), jnp.float32),
            pltpu.VMEM((tm, qw), jnp.float32),
            pltpu.VMEM((tm, d), jnp.bfloat16),
            pltpu.VMEM((tm, d), jnp.bfloat16),
            pltpu.VMEM((tm + 2 * CHUNK, kw), jnp.bfloat16),
            pltpu.VMEM((tm + 2 * CHUNK, kw), jnp.bfloat16),
        ],
        compiler_params=pltpu.CompilerParams(
            dimension_semantics=("arbitrary",),
            vmem_limit_bytes=VMEM_LIMIT_BYTES),
        name="mixer",
    )(sink, h, u, vn, q, k, v, k, v, k, v, ws, bs, ga, gb, wo)


def _rope_tables(max_seq, tm):
    half = ROT_DIM // 2
    inv_freq = ROPE_THETA ** (-jnp.arange(0, ROT_DIM, 2, dtype=jnp.float32) / ROT_DIM)
    pad = jnp.zeros((HEAD_DIM - ROT_DIM,), jnp.float32)
    freq = jnp.concatenate([inv_freq, inv_freq, pad])[None, :]
    sign = jnp.concatenate([-jnp.ones((half,)), jnp.ones((half,)), pad])[None, :]
    ang_r = jnp.arange(tm, dtype=jnp.float32)[:, None] * freq
    ang_0 = jnp.arange(0, max_seq, tm, dtype=jnp.float32)[:, None] * freq
    c0, s0 = jnp.cos(ang_0), jnp.sin(ang_0)
    zero = jnp.zeros_like(c0)
    phase = jnp.stack([c0, s0, sign * c0, sign * s0] + [zero] * (PHASE_ROWS - 4), axis=1)
    return jnp.cos(ang_r), jnp.sin(ang_r), phase.reshape(-1, HEAD_DIM)


def kernel(x_prompt, x_sample, ffn1_norm, ffn1_w_gate, ffn1_w_up, ffn1_w_down, mix_norm, w_in,
           gmlp_v_norm, gmlp_w_s, gmlp_b_s, attn_sink, out_norm_gmlp, out_norm_attn, w_out,
           ffn2_norm, ffn2_w_gate, ffn2_w_up, ffn2_w_down, final_norm):
    bf = jnp.bfloat16
    assert ffn1_norm.shape[0] == 1, "one encoder layer"
    d = x_prompt.shape[-1]
    row = lambda a: a.reshape(1, -1)
    fg = row(final_norm)
    ws = gmlp_w_s[0].astype(bf)
    bs = jnp.broadcast_to(gmlp_b_s[0][:, :, None], gmlp_b_s[0].shape + (CHUNK,))
    rope = _rope_tables(max(x_prompt.shape[1], x_sample.shape[1]), INPROJ_TM)

    def flat(x):
        return x.reshape(-1, d)

    def rest_of_layer(h, seq, wg2, wu2, wd2, w_in_bf, w_out_bf):
        u, vn, q, k, v = _inproj(h, row(mix_norm[0]), w_in_bf, row(gmlp_v_norm[0]), rope,
                                 seq=seq, tm=INPROJ_TM)
        h = _mixer(h, u, vn, q, k, v, attn_sink[0], ws, bs, row(out_norm_gmlp[0]),
                   row(out_norm_attn[0]), w_out_bf, seq=seq, tm=MIXER_TM)
        h, _ = _ffn(h, row(ffn2_norm[0]), wg2, wu2, wd2, fg, final_norm=True, tm=FFN_TM, tf=FFN_TF)
        return h

    wg1, wu1, wd1 = (w[0].astype(bf) for w in (ffn1_w_gate, ffn1_w_up, ffn1_w_down))
    later = (ffn2_w_gate[0], ffn2_w_up[0], ffn2_w_down[0], w_in[0], w_out[0])
    h_p, later_bf = _ffn(flat(x_prompt), row(ffn1_norm[0]), wg1, wu1, wd1, fg, final_norm=False,
                         tm=FFN_TM, tf=FFN_TF, cast=later)
    h_s, _ = _ffn(flat(x_sample), row(ffn1_norm[0]), wg1, wu1, wd1, fg, final_norm=False,
                  tm=FFN_TM, tf=FFN_TF)
    y_p = rest_of_layer(h_p, x_prompt.shape[1], *later_bf)
    y_s = rest_of_layer(h_s, x_sample.shape[1], *later_bf)
    return (y_p.reshape(x_prompt.shape), y_s.reshape(x_sample.shape))
```

```python
import functools

import jax
import jax.numpy as jnp
from jax import lax
from jax.experimental import pallas as pl
from jax.experimental.pallas import tpu as pltpu

HEAD_DIM = 128
GQA_GROUP = 4
GMLP_GROUPS = 8
CHUNK = 128
ROT_DIM = HEAD_DIM // 4
ROPE_THETA = 500000.0
EPS = 1e-6
NEG_INF = -1e30

VMEM_LIMIT_BYTES = 60 * 1024 * 1024

FFN_TM = 1024
FFN_TF = 512
FFN_SLAB = 256
FFN_ROW_CHUNK = 128
CAST_BLOCK_LONG = 512
INPROJ_TM = 512
PHASE_ROWS = 8
MIXER_TM = 512
MIXER_TAIL_SLABS = 2


def _rms(x, gain):
    return x * lax.rsqrt(jnp.mean(x * x, axis=-1, keepdims=True) + EPS) * gain


def _ffn_kernel(*refs, final_norm, n_cast):
    x_ref, g_ref, wg_ref, wu_ref, wd_ref, fg_ref = refs[:6]
    cast_in = refs[6:6 + n_cast]
    o_ref = refs[6 + n_cast]
    cast_out = refs[7 + n_cast:7 + 2 * n_cast]
    n_ref = refs[7 + 2 * n_cast]
    j = pl.program_id(1)

    n_chunks = x_ref.shape[0] // FFN_ROW_CHUNK

    def rows(r):
        return pl.ds(pl.multiple_of(r * FFN_ROW_CHUNK, FFN_ROW_CHUNK), FFN_ROW_CHUNK)

    @pl.when(j == 0)
    def _():
        def norm_chunk(r, carry):
            x = x_ref[rows(r), :]
            n_ref[rows(r), :] = _rms(x, g_ref[...]).astype(jnp.bfloat16)
            o_ref[rows(r), :] = x
            return carry
        lax.fori_loop(0, n_chunks, norm_chunk, 0)

    for src, dst in zip(cast_in, cast_out):
        dst[...] = src[...].astype(dst.dtype)

    n = n_ref[...]
    tf = wg_ref.shape[1]
    acts = []
    for c in range(0, tf, FFN_SLAB):
        gate = jnp.dot(n, wg_ref[:, c:c + FFN_SLAB], preferred_element_type=jnp.float32)
        up = jnp.dot(n, wu_ref[:, c:c + FFN_SLAB], preferred_element_type=jnp.float32)
        acts.append((0.5 * jax.nn.silu(gate) * up).astype(jnp.bfloat16))
    o_ref[...] += jnp.dot(jnp.concatenate(acts, axis=1), wd_ref[...],
                          preferred_element_type=jnp.float32)

    if final_norm:
        @pl.when(j == pl.num_programs(1) - 1)
        def _():
            def norm_chunk(r, carry):
                o_ref[rows(r), :] = _rms(o_ref[rows(r), :], fg_ref[...])
                return carry
            lax.fori_loop(0, n_chunks, norm_chunk, 0)


def _cast_plan(w, n_row_steps, n_col_steps):
    r, c = w.shape
    transpose = r > c
    assert min(r, c) % n_row_steps == 0 and max(r, c) % CAST_BLOCK_LONG == 0
    assert max(r, c) // CAST_BLOCK_LONG <= n_col_steps
    br = r // n_row_steps if not transpose else CAST_BLOCK_LONG
    bc = CAST_BLOCK_LONG if not transpose else c // n_row_steps
    if transpose:
        nlong = r // br
        return (br, bc), (lambda i, j: (jnp.minimum(j, nlong - 1), i))
    nlong = c // bc
    return (br, bc), (lambda i, j: (i, jnp.minimum(j, nlong - 1)))


def _ffn(x, gain, wg, wu, wd, fgain, *, final_norm, tm, tf, cast=()):
    t, d = x.shape
    dff = wg.shape[1]
    grid = (t // tm, dff // tf)
    plans = [_cast_plan(w, *grid) for w in cast]
    cast_specs = [pl.BlockSpec(blk, imap) for blk, imap in plans]
    outs = pl.pallas_call(
        functools.partial(_ffn_kernel, final_norm=final_norm, n_cast=len(cast)),
        grid=grid,
        in_specs=[
            pl.BlockSpec((tm, d), lambda i, j: (i, 0)),
            pl.BlockSpec((1, d), lambda i, j: (0, 0)),
            pl.BlockSpec((d, tf), lambda i, j: (0, j)),
            pl.BlockSpec((d, tf), lambda i, j: (0, j)),
            pl.BlockSpec((tf, d), lambda i, j: (j, 0)),
            pl.BlockSpec((1, d), lambda i, j: (0, 0)),
        ] + cast_specs,
        out_specs=[pl.BlockSpec((tm, d), lambda i, j: (i, 0))] + cast_specs,
        out_shape=[jax.ShapeDtypeStruct((t, d), jnp.float32)]
        + [jax.ShapeDtypeStruct(w.shape, jnp.bfloat16) for w in cast],
        scratch_shapes=[pltpu.VMEM((tm, d), jnp.bfloat16)],
        compiler_params=pltpu.CompilerParams(
            dimension_semantics=("arbitrary", "arbitrary"),
            vmem_limit_bytes=VMEM_LIMIT_BYTES),
        name=("ffn_final" if final_norm else "ffn") + ("_cast" if cast else ""),
    )(x, gain, wg, wu, wd, fgain, *cast)
    return outs[0], tuple(outs[1:])


def _rope(x, cos, sin, low_half):
    half = ROT_DIM // 2
    partner = jnp.where(low_half, pltpu.roll(x, HEAD_DIM - half, 1), pltpu.roll(x, half, 1))
    return x * cos + partner * sin


def _inproj_kernel(h_ref, g_ref, w_ref, vg_ref, cr_ref, sr_ref, ph_ref,
                   z_ref, *, gw, qw, kw):
    n = _rms(h_ref[...], g_ref[...]).astype(jnp.bfloat16)
    c1, c2, c3, c4 = gw, 2 * gw, 2 * gw + qw, 2 * gw + qw + kw
    u_ref, vn_ref, q_ref = z_ref.at[:, 0:c1], z_ref.at[:, c1:c2], z_ref.at[:, c2:c3]
    k_ref, v_ref = z_ref.at[:, c3:c4], z_ref.at[:, c4:c4 + kw]

    def proj(lo, hi):
        return jnp.dot(n, w_ref[:, lo:hi], preferred_element_type=jnp.float32)

    u_ref[...] = jax.nn.gelu(proj(0, c1)).astype(jnp.bfloat16)
    vn_ref[...] = _rms(jax.nn.gelu(proj(c1, c2)), vg_ref[...]).astype(jnp.bfloat16)

    cr = cr_ref[...]
    sr = sr_ref[...]
    cos = ph_ref[0:1, :] * cr - ph_ref[1:2, :] * sr
    sin = ph_ref[3:4, :] * cr + ph_ref[2:3, :] * sr
    low_half = lax.broadcasted_iota(jnp.int32, cos.shape, 1) < (ROT_DIM // 2)

    zk = proj(c3, c4)
    for hd in range(kw // HEAD_DIM):
        sl = slice(hd * HEAD_DIM, (hd + 1) * HEAD_DIM)
        k_ref[:, sl] = _rope(zk[:, sl], cos, sin, low_half).astype(jnp.bfloat16)
    zq = proj(c2, c3)
    for hd in range(qw // HEAD_DIM):
        sl = slice(hd * HEAD_DIM, (hd + 1) * HEAD_DIM)
        q_ref[:, sl] = _rope(zq[:, sl], cos, sin, low_half).astype(jnp.bfloat16)
    v_ref[...] = proj(c4, c4 + kw).astype(jnp.bfloat16)


def _inproj(h, gain, w_in, vgain, rope, *, seq, tm):
    t, d = h.shape
    gw = vgain.shape[1]
    qw = d - gw
    kw = qw // GQA_GROUP
    cols = w_in.shape[1]
    tiles_per_seq = seq // tm
    cr, sr, phase = rope
    row = lambda i: (i, 0)
    const = lambda i: (0, 0)
    return pl.pallas_call(
        functools.partial(_inproj_kernel, gw=gw, qw=qw, kw=kw),
        grid=(t // tm,),
        in_specs=[
            pl.BlockSpec((tm, d), row),
            pl.BlockSpec((1, d), const),
            pl.BlockSpec((d, cols), const, pipeline_mode=pl.Buffered(1)),
            pl.BlockSpec((1, gw), const),
            pl.BlockSpec((tm, HEAD_DIM), const),
            pl.BlockSpec((tm, HEAD_DIM), const),
            pl.BlockSpec((PHASE_ROWS, HEAD_DIM), lambda i: (i % tiles_per_seq, 0)),
        ],
        out_specs=pl.BlockSpec((tm, cols), row),
        out_shape=jax.ShapeDtypeStruct((t, cols), jnp.bfloat16),
        compiler_params=pltpu.CompilerParams(
            dimension_semantics=("arbitrary",),
            vmem_limit_bytes=VMEM_LIMIT_BYTES),
        name="inproj",
    )(h, gain, w_in, vgain, cr, sr, phase)


def _mixer_kernel(sink_ref, h_ref, z_ref, zp_ref, zn_ref, ws_ref, bs_ref,
                  ga_ref, gb_ref, wo_ref, o_ref,
                  a_ref, att_ref, m_ref, mprev_ref, kall_ref, vall_ref,
                  *, ntiles, tiles_per_seq, nblk, gw, qw, kw):
    s = pl.program_id(0)
    u_ref, vn_ref = z_ref.at[:, 0:gw], z_ref.at[:, gw:2 * gw]
    q_ref = z_ref.at[:, 2 * gw:2 * gw + qw]
    k_ref, v_ref = z_ref.at[:, 2 * gw + qw:2 * gw + qw + kw], z_ref.at[:, 2 * gw + qw + kw:]
    kp_ref, vp_ref = zp_ref.at[:, 0:kw], zp_ref.at[:, kw:]
    kn_ref, vn2_ref = zn_ref.at[:, 0:kw], zn_ref.at[:, kw:]
    n_kv = kw // HEAD_DIM
    rows = GQA_GROUP * CHUNK

    @pl.when(s == 0)
    def _():
        m_ref[...] = jnp.zeros_like(m_ref)

    mprev_ref[...] = m_ref[...]

    kall_ref[0:CHUNK] = kp_ref[...]
    kall_ref[CHUNK:(nblk + 1) * CHUNK] = k_ref[...]
    kall_ref[(nblk + 1) * CHUNK:] = kn_ref[...]
    vall_ref[0:CHUNK] = vp_ref[...]
    vall_ref[CHUNK:(nblk + 1) * CHUNK] = v_ref[...]
    vall_ref[(nblk + 1) * CHUNK:] = vn2_ref[...]

    t_in_seq = jnp.minimum(s, ntiles - 1) % tiles_per_seq
    prev_thr = jnp.where(t_in_seq == 0, 2 * CHUNK, 0)
    next_thr = jnp.where(t_in_seq == tiles_per_seq - 1, -2 * CHUNK, 0)
    rel = (lax.broadcasted_iota(jnp.int32, (rows, CHUNK), 1)
           - (lax.broadcasted_iota(jnp.int32, (rows, CHUNK), 0) & (CHUNK - 1)))
    log2e = 1.4426950408889634
    c2 = (HEAD_DIM ** -0.5) * log2e
    n_slabs = nblk * n_kv
    pw = o_ref.shape[1] // n_slabs
    gpi = GMLP_GROUPS // n_kv

    def project_slab(slab):
        ps = slice(slab * pw, (slab + 1) * pw)
        o_ref[:, ps] = h_ref[:, ps] + jnp.dot(mprev_ref[...], wo_ref[:, ps],
                                              preferred_element_type=jnp.float32)

    def scores(b, kh):
        qs = slice(b * CHUNK, (b + 1) * CHUNK)
        q = jnp.concatenate(
            [q_ref[qs, (kh * GQA_GROUP + g) * HEAD_DIM:(kh * GQA_GROUP + g + 1) * HEAD_DIM]
             for g in range(GQA_GROUP)], axis=0)
        kwin = kall_ref[b * CHUNK:(b + 3) * CHUNK, kh * HEAD_DIM:(kh + 1) * HEAD_DIM]
        return lax.dot_general(q, kwin, (((1,), (1,)), ((), ())),
                               preferred_element_type=jnp.float32)

    iters = [(b, kh) for b in range(nblk) for kh in range(n_kv)]
    sc_next = scores(*iters[0])
    for it, (b, kh) in enumerate(iters):
        pthr = prev_thr if b == 0 else 0
        nthr = next_thr if b == nblk - 1 else 0
        qs = slice(b * CHUNK, (b + 1) * CHUNK)
        sc = sc_next
        if it + 1 < len(iters):
            sc_next = scores(*iters[it + 1])
        if it < n_slabs - MIXER_TAIL_SLABS:
            project_slab(it)
        for g in range(kh * gpi, (kh + 1) * gpi):
            gs = slice(g * CHUNK, (g + 1) * CHUNK)
            mixed = jnp.dot(ws_ref[g], vn_ref[qs, gs], preferred_element_type=jnp.float32)
            a_ref[qs, gs] = u_ref[qs, gs].astype(jnp.float32) * (mixed + bs_ref[g])
        s_prev = jnp.where(rel >= pthr, sc[:, :CHUNK], NEG_INF)
        s_mid = sc[:, CHUNK:2 * CHUNK]
        s_next = jnp.where(rel <= nthr, sc[:, 2 * CHUNK:], NEG_INF)
        sink2 = jnp.concatenate(
            [jnp.full((CHUNK, 1), sink_ref[kh * GQA_GROUP + g] * log2e, jnp.float32)
             for g in range(GQA_GROUP)], axis=0)
        mx = jnp.maximum(
            jnp.max(jnp.maximum(jnp.maximum(s_prev, s_mid), s_next), axis=-1, keepdims=True) * c2,
            sink2)
        p_prev = jnp.exp2(s_prev * c2 - mx)
        p_mid = jnp.exp2(s_mid * c2 - mx)
        p_next = jnp.exp2(s_next * c2 - mx)
        denom = (jnp.sum(p_prev + p_mid + p_next, axis=-1, keepdims=True)
                 + jnp.exp2(sink2 - mx))
        p = jnp.concatenate([p_prev, p_mid, p_next], axis=1).astype(jnp.bfloat16)
        vwin = vall_ref[b * CHUNK:(b + 3) * CHUNK, kh * HEAD_DIM:(kh + 1) * HEAD_DIM]
        o = jnp.dot(p, vwin, preferred_element_type=jnp.float32) / denom
        for g in range(GQA_GROUP):
            hs = (kh * GQA_GROUP + g) * HEAD_DIM
            att_ref[qs, hs:hs + HEAD_DIM] = o[g * CHUNK:(g + 1) * CHUNK]
        if kh == n_kv - 1:
            m_ref[qs, :gw] = _rms(a_ref[qs, :], ga_ref[...]).astype(jnp.bfloat16)
            m_ref[qs, gw:] = _rms(att_ref[qs, :], gb_ref[...]).astype(jnp.bfloat16)
    for slab in range(n_slabs - MIXER_TAIL_SLABS, n_slabs):
        project_slab(slab)


def _mixer(h, z, sink, ws, bs, ga, gb, wo, *, seq, tm):
    t, d = h.shape
    gw = ga.shape[1]
    qw = gb.shape[1]
    kw = qw // GQA_GROUP
    cols = z.shape[1]
    kv_col_block, rem = divmod(2 * gw + qw, 2 * kw)
    assert rem == 0 and cols == 2 * gw + qw + 2 * kw
    nblk = tm // CHUNK
    nblocks = t // CHUNK
    ntiles = t // tm
    tiles_per_seq = seq // tm
    mix_tile = lambda s: jnp.minimum(s, ntiles - 1)
    mix_row = lambda s: (mix_tile(s), 0)
    proj_row = lambda s: (jnp.maximum(s - 1, 0), 0)
    const2 = lambda s: (0, 0)
    const3 = lambda s: (0, 0, 0)
    prev_blk = lambda s: (jnp.maximum(mix_tile(s) * nblk - 1, 0), kv_col_block)
    next_blk = lambda s: (jnp.minimum((mix_tile(s) + 1) * nblk, nblocks - 1), kv_col_block)
    return pl.pallas_call(
        functools.partial(_mixer_kernel, ntiles=ntiles, tiles_per_seq=tiles_per_seq, nblk=nblk,
                          gw=gw, qw=qw, kw=kw),
        grid=(ntiles + 1,),
        in_specs=[
            pl.BlockSpec(memory_space=pltpu.SMEM),
            pl.BlockSpec((tm, d), proj_row),
            pl.BlockSpec((tm, cols), mix_row),
            pl.BlockSpec((CHUNK, 2 * kw), prev_blk),
            pl.BlockSpec((CHUNK, 2 * kw), next_blk),
            pl.BlockSpec((GMLP_GROUPS, CHUNK, CHUNK), const3),
            pl.BlockSpec((GMLP_GROUPS, CHUNK, CHUNK), const3),
            pl.BlockSpec((1, gw), const2),
            pl.BlockSpec((1, qw), const2),
            pl.BlockSpec((d, d), const2, pipeline_mode=pl.Buffered(1)),
        ],
        out_specs=pl.BlockSpec((tm, d), proj_row),
        out_shape=jax.ShapeDtypeStruct((t, d), jnp.float32),
        scratch_shapes=[
            pltpu.VMEM((tm, gw), jnp.float32),
            pltpu.VMEM((tm, qw), jnp.float32),
            pltpu.VMEM((tm, d), jnp.bfloat16),
            pltpu.VMEM((tm, d), jnp.bfloat16),
            pltpu.VMEM((tm + 2 * CHUNK, kw), jnp.bfloat16),
            pltpu.VMEM((tm + 2 * CHUNK, kw), jnp.bfloat16),
        ],
        compiler_params=pltpu.CompilerParams(
            dimension_semantics=("arbitrary",),
            vmem_limit_bytes=VMEM_LIMIT_BYTES),
        name="mixer",
    )(sink, h, z, z, z, ws, bs, ga, gb, wo)


def _rope_tables(max_seq, tm):
    half = ROT_DIM // 2
    inv_freq = ROPE_THETA ** (-jnp.arange(0, ROT_DIM, 2, dtype=jnp.float32) / ROT_DIM)
    pad = jnp.zeros((HEAD_DIM - ROT_DIM,), jnp.float32)
    freq = jnp.concatenate([inv_freq, inv_freq, pad])[None, :]
    sign = jnp.concatenate([-jnp.ones((half,)), jnp.ones((half,)), pad])[None, :]
    ang_r = jnp.arange(tm, dtype=jnp.float32)[:, None] * freq
    ang_0 = jnp.arange(0, max_seq, tm, dtype=jnp.float32)[:, None] * freq
    c0, s0 = jnp.cos(ang_0), jnp.sin(ang_0)
    zero = jnp.zeros_like(c0)
    phase = jnp.stack([c0, s0, sign * c0, sign * s0] + [zero] * (PHASE_ROWS - 4), axis=1)
    return jnp.cos(ang_r), jnp.sin(ang_r), phase.reshape(-1, HEAD_DIM)


def kernel(x_prompt, x_sample, ffn1_norm, ffn1_w_gate, ffn1_w_up, ffn1_w_down, mix_norm, w_in,
           gmlp_v_norm, gmlp_w_s, gmlp_b_s, attn_sink, out_norm_gmlp, out_norm_attn, w_out,
           ffn2_norm, ffn2_w_gate, ffn2_w_up, ffn2_w_down, final_norm):
    bf = jnp.bfloat16
    assert ffn1_norm.shape[0] == 1, "one encoder layer"
    d = x_prompt.shape[-1]
    row = lambda a: a.reshape(1, -1)
    fg = row(final_norm)
    ws = gmlp_w_s[0].astype(bf)
    bs = jnp.broadcast_to(gmlp_b_s[0][:, :, None], gmlp_b_s[0].shape + (CHUNK,))
    rope = _rope_tables(max(x_prompt.shape[1], x_sample.shape[1]), INPROJ_TM)

    def flat(x):
        return x.reshape(-1, d)

    def rest_of_layer(h, seq, wg2, wu2, wd2, w_in_bf, w_out_bf):
        z = _inproj(h, row(mix_norm[0]), w_in_bf, row(gmlp_v_norm[0]), rope, seq=seq, tm=INPROJ_TM)
        h = _mixer(h, z, attn_sink[0], ws, bs, row(out_norm_gmlp[0]),
                   row(out_norm_attn[0]), w_out_bf, seq=seq, tm=MIXER_TM)
        h, _ = _ffn(h, row(ffn2_norm[0]), wg2, wu2, wd2, fg, final_norm=True, tm=FFN_TM, tf=FFN_TF)
        return h

    wg1, wu1, wd1 = (w[0].astype(bf) for w in (ffn1_w_gate, ffn1_w_up, ffn1_w_down))
    later = (ffn2_w_gate[0], ffn2_w_up[0], ffn2_w_down[0], w_in[0], w_out[0])
    h_p, later_bf = _ffn(flat(x_prompt), row(ffn1_norm[0]), wg1, wu1, wd1, fg, final_norm=False,
                         tm=FFN_TM, tf=FFN_TF, cast=later)
    h_s, _ = _ffn(flat(x_sample), row(ffn1_norm[0]), wg1, wu1, wd1, fg, final_norm=False,
                  tm=FFN_TM, tf=FFN_TF)
    y_p = rest_of_layer(h_p, x_prompt.shape[1], *later_bf)
    y_s = rest_of_layer(h_s, x_sample.shape[1], *later_bf)
    return (y_p.reshape(x_prompt.shape), y_s.reshape(x_sample.shape))
```

```python
import functools

import jax
import jax.numpy as jnp
from jax import lax
from jax.experimental import pallas as pl
from jax.experimental.pallas import tpu as pltpu

HEAD_DIM = 128
GQA_GROUP = 4
GMLP_GROUPS = 8
CHUNK = 128
ROT_DIM = HEAD_DIM // 4
ROPE_THETA = 500000.0
EPS = 1e-6
NEG_INF = -1e30

VMEM_LIMIT_BYTES = 60 * 1024 * 1024

FFN_TM = 1024
FFN_TF = 512
FFN_SLAB = 256
FFN_ROW_CHUNK = 256
CAST_BLOCK_LONG = 512
INPROJ_TM = 1024
PHASE_ROWS = 8
MIXER_TM = 512
MIXER_TAIL_SLABS = 2


def _rms(x, gain):
    return x * lax.rsqrt(jnp.mean(x * x, axis=-1, keepdims=True) + EPS) * gain


def _ffn_kernel(*refs, final_norm, n_cast):
    x_ref, g_ref, wg_ref, wu_ref, wd_ref, fg_ref = refs[:6]
    cast_in = refs[6:6 + n_cast]
    o_ref = refs[6 + n_cast]
    cast_out = refs[7 + n_cast:7 + 2 * n_cast]
    n_ref = refs[7 + 2 * n_cast]
    j = pl.program_id(1)

    n_chunks = x_ref.shape[0] // FFN_ROW_CHUNK

    def rows(r):
        return pl.ds(pl.multiple_of(r * FFN_ROW_CHUNK, FFN_ROW_CHUNK), FFN_ROW_CHUNK)

    @pl.when(j == 0)
    def _():
        def norm_chunk(r, carry):
            x = x_ref[rows(r), :]
            n_ref[rows(r), :] = _rms(x, g_ref[...]).astype(jnp.bfloat16)
            o_ref[rows(r), :] = x
            return carry
        lax.fori_loop(0, n_chunks, norm_chunk, 0)

    for src, dst in zip(cast_in, cast_out):
        dst[...] = src[...].astype(dst.dtype)

    n = n_ref[...]
    tf = wg_ref.shape[1]
    acts = []
    for c in range(0, tf, FFN_SLAB):
        gate = jnp.dot(n, wg_ref[:, c:c + FFN_SLAB], preferred_element_type=jnp.float32)
        up = jnp.dot(n, wu_ref[:, c:c + FFN_SLAB], preferred_element_type=jnp.float32)
        acts.append((0.5 * jax.nn.silu(gate) * up).astype(jnp.bfloat16))
    o_ref[...] += jnp.dot(jnp.concatenate(acts, axis=1), wd_ref[...],
                          preferred_element_type=jnp.float32)

    if final_norm:
        @pl.when(j == pl.num_programs(1) - 1)
        def _():
            def norm_chunk(r, carry):
                o_ref[rows(r), :] = _rms(o_ref[rows(r), :], fg_ref[...])
                return carry
            lax.fori_loop(0, n_chunks, norm_chunk, 0)


def _cast_plan(w, n_row_steps, n_col_steps):
    r, c = w.shape
    transpose = r > c
    assert min(r, c) % n_row_steps == 0 and max(r, c) % CAST_BLOCK_LONG == 0
    assert max(r, c) // CAST_BLOCK_LONG <= n_col_steps
    br = r // n_row_steps if not transpose else CAST_BLOCK_LONG
    bc = CAST_BLOCK_LONG if not transpose else c // n_row_steps
    if transpose:
        nlong = r // br
        return (br, bc), (lambda i, j: (jnp.minimum(j, nlong - 1), i))
    nlong = c // bc
    return (br, bc), (lambda i, j: (i, jnp.minimum(j, nlong - 1)))


def _ffn(x, gain, wg, wu, wd, fgain, *, final_norm, tm, tf, cast=()):
    t, d = x.shape
    dff = wg.shape[1]
    grid = (t // tm, dff // tf)
    plans = [_cast_plan(w, *grid) for w in cast]
    cast_specs = [pl.BlockSpec(blk, imap) for blk, imap in plans]
    outs = pl.pallas_call(
        functools.partial(_ffn_kernel, final_norm=final_norm, n_cast=len(cast)),
        grid=grid,
        in_specs=[
            pl.BlockSpec((tm, d), lambda i, j: (i, 0)),
            pl.BlockSpec((1, d), lambda i, j: (0, 0)),
            pl.BlockSpec((d, tf), lambda i, j: (0, j)),
            pl.BlockSpec((d, tf), lambda i, j: (0, j)),
            pl.BlockSpec((tf, d), lambda i, j: (j, 0)),
            pl.BlockSpec((1, d), lambda i, j: (0, 0)),
        ] + cast_specs,
        out_specs=[pl.BlockSpec((tm, d), lambda i, j: (i, 0))] + cast_specs,
        out_shape=[jax.ShapeDtypeStruct((t, d), jnp.float32)]
        + [jax.ShapeDtypeStruct(w.shape, jnp.bfloat16) for w in cast],
        scratch_shapes=[pltpu.VMEM((tm, d), jnp.bfloat16)],
        compiler_params=pltpu.CompilerParams(
            dimension_semantics=("arbitrary", "arbitrary"),
            vmem_limit_bytes=VMEM_LIMIT_BYTES),
        name=("ffn_final" if final_norm else "ffn") + ("_cast" if cast else ""),
    )(x, gain, wg, wu, wd, fgain, *cast)
    return outs[0], tuple(outs[1:])


def _rope(x, cos, sin, low_half):
    half = ROT_DIM // 2
    partner = jnp.where(low_half, pltpu.roll(x, HEAD_DIM - half, 1), pltpu.roll(x, half, 1))
    return x * cos + partner * sin


def _inproj_kernel(h_ref, g_ref, w_ref, vg_ref, cr_ref, sr_ref, ph_ref,
                   z_ref, *, gw, qw, kw):
    n = _rms(h_ref[...], g_ref[...]).astype(jnp.bfloat16)
    c1, c2, c3, c4 = gw, 2 * gw, 2 * gw + qw, 2 * gw + qw + kw
    u_ref, vn_ref, q_ref = z_ref.at[:, 0:c1], z_ref.at[:, c1:c2], z_ref.at[:, c2:c3]
    k_ref, v_ref = z_ref.at[:, c3:c4], z_ref.at[:, c4:c4 + kw]

    def proj(lo, hi):
        return jnp.dot(n, w_ref[:, lo:hi], preferred_element_type=jnp.float32)

    u_ref[...] = jax.nn.gelu(proj(0, c1)).astype(jnp.bfloat16)
    vn_ref[...] = _rms(jax.nn.gelu(proj(c1, c2)), vg_ref[...]).astype(jnp.bfloat16)

    cr = cr_ref[...]
    sr = sr_ref[...]
    cos = ph_ref[0:1, :] * cr - ph_ref[1:2, :] * sr
    sin = ph_ref[3:4, :] * cr + ph_ref[2:3, :] * sr
    low_half = lax.broadcasted_iota(jnp.int32, cos.shape, 1) < (ROT_DIM // 2)

    zk = proj(c3, c4)
    for hd in range(kw // HEAD_DIM):
        sl = slice(hd * HEAD_DIM, (hd + 1) * HEAD_DIM)
        k_ref[:, sl] = _rope(zk[:, sl], cos, sin, low_half).astype(jnp.bfloat16)
    zq = proj(c2, c3)
    for hd in range(qw // HEAD_DIM):
        sl = slice(hd * HEAD_DIM, (hd + 1) * HEAD_DIM)
        q_ref[:, sl] = _rope(zq[:, sl], cos, sin, low_half).astype(jnp.bfloat16)
    v_ref[...] = proj(c4, c4 + kw).astype(jnp.bfloat16)


def _inproj(h, gain, w_in, vgain, rope, *, seq, tm):
    t, d = h.shape
    gw = vgain.shape[1]
    qw = d - gw
    kw = qw // GQA_GROUP
    cols = w_in.shape[1]
    tiles_per_seq = seq // tm
    cr, sr, phase = rope
    row = lambda i: (i, 0)
    const = lambda i: (0, 0)
    return pl.pallas_call(
        functools.partial(_inproj_kernel, gw=gw, qw=qw, kw=kw),
        grid=(t // tm,),
        in_specs=[
            pl.BlockSpec((tm, d), row),
            pl.BlockSpec((1, d), const),
            pl.BlockSpec((d, cols), const, pipeline_mode=pl.Buffered(1)),
            pl.BlockSpec((1, gw), const),
            pl.BlockSpec((tm, HEAD_DIM), const),
            pl.BlockSpec((tm, HEAD_DIM), const),
            pl.BlockSpec((PHASE_ROWS, HEAD_DIM), lambda i: (i % tiles_per_seq, 0)),
        ],
        out_specs=pl.BlockSpec((tm, cols), row),
        out_shape=jax.ShapeDtypeStruct((t, cols), jnp.bfloat16),
        compiler_params=pltpu.CompilerParams(
            dimension_semantics=("arbitrary",),
            vmem_limit_bytes=VMEM_LIMIT_BYTES),
        name="inproj",
    )(h, gain, w_in, vgain, cr, sr, phase)


def _mixer_kernel(sink_ref, h_ref, z_ref, zp_ref, zn_ref, ws_ref, bs_ref,
                  ga_ref, gb_ref, wo_ref, o_ref,
                  a_ref, att_ref, m_ref, mprev_ref, kall_ref, vall_ref,
                  *, ntiles, tiles_per_seq, nblk, gw, qw, kw):
    s = pl.program_id(0)
    u_ref, vn_ref = z_ref.at[:, 0:gw], z_ref.at[:, gw:2 * gw]
    q_ref = z_ref.at[:, 2 * gw:2 * gw + qw]
    k_ref, v_ref = z_ref.at[:, 2 * gw + qw:2 * gw + qw + kw], z_ref.at[:, 2 * gw + qw + kw:]
    kp_ref, vp_ref = zp_ref.at[:, 0:kw], zp_ref.at[:, kw:]
    kn_ref, vn2_ref = zn_ref.at[:, 0:kw], zn_ref.at[:, kw:]
    n_kv = kw // HEAD_DIM
    rows = GQA_GROUP * CHUNK

    @pl.when(s == 0)
    def _():
        m_ref[...] = jnp.zeros_like(m_ref)

    mprev_ref[...] = m_ref[...]

    kall_ref[0:CHUNK] = kp_ref[...]
    kall_ref[CHUNK:(nblk + 1) * CHUNK] = k_ref[...]
    kall_ref[(nblk + 1) * CHUNK:] = kn_ref[...]
    vall_ref[0:CHUNK] = vp_ref[...]
    vall_ref[CHUNK:(nblk + 1) * CHUNK] = v_ref[...]
    vall_ref[(nblk + 1) * CHUNK:] = vn2_ref[...]

    t_in_seq = jnp.minimum(s, ntiles - 1) % tiles_per_seq
    prev_thr = jnp.where(t_in_seq == 0, 2 * CHUNK, 0)
    next_thr = jnp.where(t_in_seq == tiles_per_seq - 1, -2 * CHUNK, 0)
    rel = (lax.broadcasted_iota(jnp.int32, (rows, CHUNK), 1)
           - (lax.broadcasted_iota(jnp.int32, (rows, CHUNK), 0) & (CHUNK - 1)))
    log2e = 1.4426950408889634
    c2 = (HEAD_DIM ** -0.5) * log2e
    n_slabs = nblk * n_kv
    pw = o_ref.shape[1] // n_slabs
    gpi = GMLP_GROUPS // n_kv

    def project_slab(slab):
        ps = slice(slab * pw, (slab + 1) * pw)
        o_ref[:, ps] = h_ref[:, ps] + jnp.dot(mprev_ref[...], wo_ref[:, ps],
                                              preferred_element_type=jnp.float32)

    def scores(b, kh):
        qs = slice(b * CHUNK, (b + 1) * CHUNK)
        q = jnp.concatenate(
            [q_ref[qs, (kh * GQA_GROUP + g) * HEAD_DIM:(kh * GQA_GROUP + g + 1) * HEAD_DIM]
             for g in range(GQA_GROUP)], axis=0)
        kwin = kall_ref[b * CHUNK:(b + 3) * CHUNK, kh * HEAD_DIM:(kh + 1) * HEAD_DIM]
        return lax.dot_general(q, kwin, (((1,), (1,)), ((), ())),
                               preferred_element_type=jnp.float32)

    iters = [(b, kh) for b in range(nblk) for kh in range(n_kv)]
    sc_next = scores(*iters[0])
    for it, (b, kh) in enumerate(iters):
        pthr = prev_thr if b == 0 else 0
        nthr = next_thr if b == nblk - 1 else 0
        qs = slice(b * CHUNK, (b + 1) * CHUNK)
        sc = sc_next
        if it + 1 < len(iters):
            sc_next = scores(*iters[it + 1])
        if it < n_slabs - MIXER_TAIL_SLABS:
            project_slab(it)
        for g in range(kh * gpi, (kh + 1) * gpi):
            gs = slice(g * CHUNK, (g + 1) * CHUNK)
            mixed = jnp.dot(ws_ref[g], vn_ref[qs, gs], preferred_element_type=jnp.float32)
            a_ref[qs, gs] = u_ref[qs, gs].astype(jnp.float32) * (mixed + bs_ref[g])
        s_prev = jnp.where(rel >= pthr, sc[:, :CHUNK], NEG_INF)
        s_mid = sc[:, CHUNK:2 * CHUNK]
        s_next = jnp.where(rel <= nthr, sc[:, 2 * CHUNK:], NEG_INF)
        sink2 = jnp.concatenate(
            [jnp.full((CHUNK, 1), sink_ref[kh * GQA_GROUP + g] * log2e, jnp.float32)
             for g in range(GQA_GROUP)], axis=0)
        mx = jnp.maximum(
            jnp.max(jnp.maximum(jnp.maximum(s_prev, s_mid), s_next), axis=-1, keepdims=True) * c2,
            sink2)
        p_prev = jnp.exp2(s_prev * c2 - mx)
        p_mid = jnp.exp2(s_mid * c2 - mx)
        p_next = jnp.exp2(s_next * c2 - mx)
        denom = (jnp.sum(p_prev + p_mid + p_next, axis=-1, keepdims=True)
                 + jnp.exp2(sink2 - mx))
        p = jnp.concatenate([p_prev, p_mid, p_next], axis=1).astype(jnp.bfloat16)
        vwin = vall_ref[b * CHUNK:(b + 3) * CHUNK, kh * HEAD_DIM:(kh + 1) * HEAD_DIM]
        o = jnp.dot(p, vwin, preferred_element_type=jnp.float32) / denom
        for g in range(GQA_GROUP):
            hs = (kh * GQA_GROUP + g) * HEAD_DIM
            att_ref[qs, hs:hs + HEAD_DIM] = o[g * CHUNK:(g + 1) * CHUNK]
        if kh == n_kv - 1:
            m_ref[qs, :gw] = _rms(a_ref[qs, :], ga_ref[...]).astype(jnp.bfloat16)
            m_ref[qs, gw:] = _rms(att_ref[qs, :], gb_ref[...]).astype(jnp.bfloat16)
    for slab in range(n_slabs - MIXER_TAIL_SLABS, n_slabs):
        project_slab(slab)


def _mixer(h, z, sink, ws, bs, ga, gb, wo, *, seq, tm):
    t, d = h.shape
    gw = ga.shape[1]
    qw = gb.shape[1]
    kw = qw // GQA_GROUP
    cols = z.shape[1]
    kv_col_block, rem = divmod(2 * gw + qw, 2 * kw)
    assert rem == 0 and cols == 2 * gw + qw + 2 * kw
    nblk = tm // CHUNK
    nblocks = t // CHUNK
    ntiles = t // tm
    tiles_per_seq = seq // tm
    mix_tile = lambda s: jnp.minimum(s, ntiles - 1)
    mix_row = lambda s: (mix_tile(s), 0)
    proj_row = lambda s: (jnp.maximum(s - 1, 0), 0)
    const2 = lambda s: (0, 0)
    const3 = lambda s: (0, 0, 0)
    prev_blk = lambda s: (jnp.maximum(mix_tile(s) * nblk - 1, 0), kv_col_block)
    next_blk = lambda s: (jnp.minimum((mix_tile(s) + 1) * nblk, nblocks - 1), kv_col_block)
    return pl.pallas_call(
        functools.partial(_mixer_kernel, ntiles=ntiles, tiles_per_seq=tiles_per_seq, nblk=nblk,
                          gw=gw, qw=qw, kw=kw),
        grid=(ntiles + 1,),
        in_specs=[
            pl.BlockSpec(memory_space=pltpu.SMEM),
            pl.BlockSpec((tm, d), proj_row),
            pl.BlockSpec((tm, cols), mix_row),
            pl.BlockSpec((CHUNK, 2 * kw), prev_blk),
            pl.BlockSpec((CHUNK, 2 * kw), next_blk),
            pl.BlockSpec((GMLP_GROUPS, CHUNK, CHUNK), const3),
            pl.BlockSpec((GMLP_GROUPS, CHUNK, CHUNK), const3),
            pl.BlockSpec((1, gw), const2),
            pl.BlockSpec((1, qw), const2),
            pl.BlockSpec((d, d), const2, pipeline_mode=pl.Buffered(1)),
        ],
        out_specs=pl.BlockSpec((tm, d), proj_row),
        out_shape=jax.ShapeDtypeStruct((t, d), jnp.float32),
        scratch_shapes=[
            pltpu.VMEM((tm, gw), jnp.float32),
            pltpu.VMEM((tm, qw), jnp.float32),
            pltpu.VMEM((tm, d), jnp.bfloat16),
            pltpu.VMEM((tm, d), jnp.bfloat16),
            pltpu.VMEM((tm + 2 * CHUNK, kw), jnp.bfloat16),
            pltpu.VMEM((tm + 2 * CHUNK, kw), jnp.bfloat16),
        ],
        compiler_params=pltpu.CompilerParams(
            dimension_semantics=("arbitrary",),
            vmem_limit_bytes=VMEM_LIMIT_BYTES),
        name="mixer",
    )(sink, h, z, z, z, ws, bs, ga, gb, wo)


def _rope_tables(max_seq, tm):
    half = ROT_DIM // 2
    inv_freq = ROPE_THETA ** (-jnp.arange(0, ROT_DIM, 2, dtype=jnp.float32) / ROT_DIM)
    pad = jnp.zeros((HEAD_DIM - ROT_DIM,), jnp.float32)
    freq = jnp.concatenate([inv_freq, inv_freq, pad])[None, :]
    sign = jnp.concatenate([-jnp.ones((half,)), jnp.ones((half,)), pad])[None, :]
    ang_r = jnp.arange(tm, dtype=jnp.float32)[:, None] * freq
    ang_0 = jnp.arange(0, max_seq, tm, dtype=jnp.float32)[:, None] * freq
    c0, s0 = jnp.cos(ang_0), jnp.sin(ang_0)
    zero = jnp.zeros_like(c0)
    phase = jnp.stack([c0, s0, sign * c0, sign * s0] + [zero] * (PHASE_ROWS - 4), axis=1)
    return jnp.cos(ang_r), jnp.sin(ang_r), phase.reshape(-1, HEAD_DIM)


def kernel(x_prompt, x_sample, ffn1_norm, ffn1_w_gate, ffn1_w_up, ffn1_w_down, mix_norm, w_in,
           gmlp_v_norm, gmlp_w_s, gmlp_b_s, attn_sink, out_norm_gmlp, out_norm_attn, w_out,
           ffn2_norm, ffn2_w_gate, ffn2_w_up, ffn2_w_down, final_norm):
    bf = jnp.bfloat16
    assert ffn1_norm.shape[0] == 1, "one encoder layer"
    d = x_prompt.shape[-1]
    row = lambda a: a.reshape(1, -1)
    fg = row(final_norm)
    ws = gmlp_w_s[0].astype(bf)
    bs = jnp.broadcast_to(gmlp_b_s[0][:, :, None], gmlp_b_s[0].shape + (CHUNK,))
    rope = _rope_tables(max(x_prompt.shape[1], x_sample.shape[1]), INPROJ_TM)

    def flat(x):
        return x.reshape(-1, d)

    def rest_of_layer(h, seq, wg2, wu2, wd2, w_in_bf, w_out_bf):
        z = _inproj(h, row(mix_norm[0]), w_in_bf, row(gmlp_v_norm[0]), rope, seq=seq, tm=INPROJ_TM)
        h = _mixer(h, z, attn_sink[0], ws, bs, row(out_norm_gmlp[0]),
                   row(out_norm_attn[0]), w_out_bf, seq=seq, tm=MIXER_TM)
        h, _ = _ffn(h, row(ffn2_norm[0]), wg2, wu2, wd2, fg, final_norm=True, tm=FFN_TM, tf=FFN_TF)
        return h

    wg1, wu1, wd1 = (w[0].astype(bf) for w in (ffn1_w_gate, ffn1_w_up, ffn1_w_down))
    later = (ffn2_w_gate[0], ffn2_w_up[0], ffn2_w_down[0], w_in[0], w_out[0])
    h_p, later_bf = _ffn(flat(x_prompt), row(ffn1_norm[0]), wg1, wu1, wd1, fg, final_norm=False,
                         tm=FFN_TM, tf=FFN_TF, cast=later)
    h_s, _ = _ffn(flat(x_sample), row(ffn1_norm[0]), wg1, wu1, wd1, fg, final_norm=False,
                  tm=FFN_TM, tf=FFN_TF)
    y_p = rest_of_layer(h_p, x_prompt.shape[1], *later_bf)
    y_s = rest_of_layer(h_s, x_sample.shape[1], *later_bf)
    return (y_p.reshape(x_prompt.shape), y_s.reshape(x_sample.shape))
```

```python
import functools

import jax
import jax.numpy as jnp
from jax import lax
from jax.experimental import pallas as pl
from jax.experimental.pallas import tpu as pltpu

HEAD_DIM = 128
GQA_GROUP = 4
GMLP_GROUPS = 8
CHUNK = 128
ROT_DIM = HEAD_DIM // 4
ROPE_THETA = 500000.0
EPS = 1e-6
NEG_INF = -1e30

VMEM_LIMIT_BYTES = 60 * 1024 * 1024

FFN_TM = 1024
FFN_TF = 512
FFN_SLAB = 256
FFN_ROW_CHUNK = 256
CAST_BLOCK_LONG = 512
INPROJ_TM = 1024
PHASE_ROWS = 8
MIXER_TM = 512
MIXER_TAIL_SLABS = 2


def _rms(x, gain):
    return x * lax.rsqrt(jnp.mean(x * x, axis=-1, keepdims=True) + EPS) * gain


def _ffn_kernel(*refs, final_norm, n_cast):
    x_ref, g_ref, wg_ref, wu_ref, wd_ref, fg_ref = refs[:6]
    cast_in = refs[6:6 + n_cast]
    o_ref = refs[6 + n_cast]
    cast_out = refs[7 + n_cast:7 + 2 * n_cast]
    n_ref = refs[7 + 2 * n_cast]
    j = pl.program_id(1)

    n_chunks = x_ref.shape[0] // FFN_ROW_CHUNK

    def rows(r):
        return pl.ds(pl.multiple_of(r * FFN_ROW_CHUNK, FFN_ROW_CHUNK), FFN_ROW_CHUNK)

    @pl.when(j == 0)
    def _():
        def norm_chunk(r, carry):
            x = x_ref[rows(r), :]
            n_ref[rows(r), :] = _rms(x, g_ref[...]).astype(jnp.bfloat16)
            o_ref[rows(r), :] = x
            return carry
        lax.fori_loop(0, n_chunks, norm_chunk, 0)

    for src, dst in zip(cast_in, cast_out):
        dst[...] = src[...].astype(dst.dtype)

    n = n_ref[...]
    tf = wg_ref.shape[1]
    acts = []
    for c in range(0, tf, FFN_SLAB):
        gate = jnp.dot(n, wg_ref[:, c:c + FFN_SLAB], preferred_element_type=jnp.float32)
        up = jnp.dot(n, wu_ref[:, c:c + FFN_SLAB], preferred_element_type=jnp.float32)
        acts.append((0.5 * jax.nn.silu(gate) * up).astype(jnp.bfloat16))
    o_ref[...] += jnp.dot(jnp.concatenate(acts, axis=1), wd_ref[...],
                          preferred_element_type=jnp.float32)

    if final_norm:
        @pl.when(j == pl.num_programs(1) - 1)
        def _():
            def norm_chunk(r, carry):
                o_ref[rows(r), :] = _rms(o_ref[rows(r), :], fg_ref[...])
                return carry
            lax.fori_loop(0, n_chunks, norm_chunk, 0)


def _cast_plan(w, n_row_steps, n_col_steps):
    r, c = w.shape
    transpose = r > c
    assert min(r, c) % n_row_steps == 0 and max(r, c) % CAST_BLOCK_LONG == 0
    assert max(r, c) // CAST_BLOCK_LONG <= n_col_steps
    br = r // n_row_steps if not transpose else CAST_BLOCK_LONG
    bc = CAST_BLOCK_LONG if not transpose else c // n_row_steps
    if transpose:
        nlong = r // br
        return (br, bc), (lambda i, j: (jnp.minimum(j, nlong - 1), i))
    nlong = c // bc
    return (br, bc), (lambda i, j: (i, jnp.minimum(j, nlong - 1)))


def _ffn(x, gain, wg, wu, wd, fgain, *, final_norm, tm, tf, cast=()):
    t, d = x.shape
    dff = wg.shape[1]
    grid = (t // tm, dff // tf)
    plans = [_cast_plan(w, *grid) for w in cast]
    cast_specs = [pl.BlockSpec(blk, imap) for blk, imap in plans]
    outs = pl.pallas_call(
        functools.partial(_ffn_kernel, final_norm=final_norm, n_cast=len(cast)),
        grid=grid,
        in_specs=[
            pl.BlockSpec((tm, d), lambda i, j: (i, 0)),
            pl.BlockSpec((1, d), lambda i, j: (0, 0)),
            pl.BlockSpec((d, tf), lambda i, j: (0, j)),
            pl.BlockSpec((d, tf), lambda i, j: (0, j)),
            pl.BlockSpec((tf, d), lambda i, j: (j, 0)),
            pl.BlockSpec((1, d), lambda i, j: (0, 0)),
        ] + cast_specs,
        out_specs=[pl.BlockSpec((tm, d), lambda i, j: (i, 0))] + cast_specs,
        out_shape=[jax.ShapeDtypeStruct((t, d), jnp.float32)]
        + [jax.ShapeDtypeStruct(w.shape, jnp.bfloat16) for w in cast],
        scratch_shapes=[pltpu.VMEM((tm, d), jnp.bfloat16)],
        compiler_params=pltpu.CompilerParams(
            dimension_semantics=("arbitrary", "arbitrary"),
            vmem_limit_bytes=VMEM_LIMIT_BYTES),
        name=("ffn_final" if final_norm else "ffn") + ("_cast" if cast else ""),
    )(x, gain, wg, wu, wd, fgain, *cast)
    return outs[0], tuple(outs[1:])


def _rope(x, cos, sin, low_half):
    half = ROT_DIM // 2
    partner = jnp.where(low_half, pltpu.roll(x, HEAD_DIM - half, 1), pltpu.roll(x, half, 1))
    return x * cos + partner * sin


def _inproj_kernel(h_ref, g_ref, w_ref, vg_ref, cr_ref, sr_ref, ph_ref,
                   z_ref, *, gw, qw, kw):
    n = _rms(h_ref[...], g_ref[...]).astype(jnp.bfloat16)
    c1, c2, c3, c4 = gw, 2 * gw, 2 * gw + qw, 2 * gw + qw + kw
    u_ref, vn_ref, q_ref = z_ref.at[:, 0:c1], z_ref.at[:, c1:c2], z_ref.at[:, c2:c3]
    k_ref, v_ref = z_ref.at[:, c3:c4], z_ref.at[:, c4:c4 + kw]

    def proj(lo, hi):
        return jnp.dot(n, w_ref[:, lo:hi], preferred_element_type=jnp.float32)

    cr = cr_ref[...]
    sr = sr_ref[...]
    cos = ph_ref[0:1, :] * cr - ph_ref[1:2, :] * sr
    sin = ph_ref[3:4, :] * cr + ph_ref[2:3, :] * sr
    low_half = lax.broadcasted_iota(jnp.int32, cos.shape, 1) < (ROT_DIM // 2)

    zk = proj(c3, c4)
    for hd in range(kw // HEAD_DIM):
        sl = slice(hd * HEAD_DIM, (hd + 1) * HEAD_DIM)
        k_ref[:, sl] = _rope(zk[:, sl], cos, sin, low_half).astype(jnp.bfloat16)
    zq = proj(c2, c3)
    for hd in range(qw // HEAD_DIM):
        sl = slice(hd * HEAD_DIM, (hd + 1) * HEAD_DIM)
        q_ref[:, sl] = _rope(zq[:, sl], cos, sin, low_half).astype(jnp.bfloat16)
    u_ref[...] = jax.nn.gelu(proj(0, c1)).astype(jnp.bfloat16)
    vn_ref[...] = _rms(jax.nn.gelu(proj(c1, c2)), vg_ref[...]).astype(jnp.bfloat16)
    v_ref[...] = proj(c4, c4 + kw).astype(jnp.bfloat16)


def _inproj(h, gain, w_in, vgain, rope, *, seq, tm):
    t, d = h.shape
    gw = vgain.shape[1]
    qw = d - gw
    kw = qw // GQA_GROUP
    cols = w_in.shape[1]
    tiles_per_seq = seq // tm
    cr, sr, phase = rope
    row = lambda i: (i, 0)
    const = lambda i: (0, 0)
    return pl.pallas_call(
        functools.partial(_inproj_kernel, gw=gw, qw=qw, kw=kw),
        grid=(t // tm,),
        in_specs=[
            pl.BlockSpec((tm, d), row),
            pl.BlockSpec((1, d), const),
            pl.BlockSpec((d, cols), const, pipeline_mode=pl.Buffered(1)),
            pl.BlockSpec((1, gw), const),
            pl.BlockSpec((tm, HEAD_DIM), const),
            pl.BlockSpec((tm, HEAD_DIM), const),
            pl.BlockSpec((PHASE_ROWS, HEAD_DIM), lambda i: (i % tiles_per_seq, 0)),
        ],
        out_specs=pl.BlockSpec((tm, cols), row),
        out_shape=jax.ShapeDtypeStruct((t, cols), jnp.bfloat16),
        compiler_params=pltpu.CompilerParams(
            dimension_semantics=("arbitrary",),
            vmem_limit_bytes=VMEM_LIMIT_BYTES),
        name="inproj",
    )(h, gain, w_in, vgain, cr, sr, phase)


def _mixer_kernel(sink_ref, h_ref, z_ref, zp_ref, zn_ref, ws_ref, bs_ref,
                  ga_ref, gb_ref, wo_ref, o_ref,
                  a_ref, att_ref, m_ref, mprev_ref, kall_ref, vall_ref,
                  *, ntiles, tiles_per_seq, nblk, gw, qw, kw):
    s = pl.program_id(0)
    u_ref, vn_ref = z_ref.at[:, 0:gw], z_ref.at[:, gw:2 * gw]
    q_ref = z_ref.at[:, 2 * gw:2 * gw + qw]
    k_ref, v_ref = z_ref.at[:, 2 * gw + qw:2 * gw + qw + kw], z_ref.at[:, 2 * gw + qw + kw:]
    kp_ref, vp_ref = zp_ref.at[:, 0:kw], zp_ref.at[:, kw:]
    kn_ref, vn2_ref = zn_ref.at[:, 0:kw], zn_ref.at[:, kw:]
    n_kv = kw // HEAD_DIM
    rows = GQA_GROUP * CHUNK

    @pl.when(s == 0)
    def _():
        m_ref[...] = jnp.zeros_like(m_ref)

    mprev_ref[...] = m_ref[...]

    kall_ref[0:CHUNK] = kp_ref[...]
    kall_ref[CHUNK:(nblk + 1) * CHUNK] = k_ref[...]
    kall_ref[(nblk + 1) * CHUNK:] = kn_ref[...]
    vall_ref[0:CHUNK] = vp_ref[...]
    vall_ref[CHUNK:(nblk + 1) * CHUNK] = v_ref[...]
    vall_ref[(nblk + 1) * CHUNK:] = vn2_ref[...]

    t_in_seq = jnp.minimum(s, ntiles - 1) % tiles_per_seq
    prev_thr = jnp.where(t_in_seq == 0, 2 * CHUNK, 0)
    next_thr = jnp.where(t_in_seq == tiles_per_seq - 1, -2 * CHUNK, 0)
    rel = (lax.broadcasted_iota(jnp.int32, (rows, CHUNK), 1)
           - (lax.broadcasted_iota(jnp.int32, (rows, CHUNK), 0) & (CHUNK - 1)))
    log2e = 1.4426950408889634
    c2 = (HEAD_DIM ** -0.5) * log2e
    n_slabs = nblk * n_kv
    pw = o_ref.shape[1] // n_slabs
    gpi = GMLP_GROUPS // n_kv

    def project_slab(slab):
        ps = slice(slab * pw, (slab + 1) * pw)
        o_ref[:, ps] = h_ref[:, ps] + jnp.dot(mprev_ref[...], wo_ref[:, ps],
                                              preferred_element_type=jnp.float32)

    def scores(b, kh):
        qs = slice(b * CHUNK, (b + 1) * CHUNK)
        q = jnp.concatenate(
            [q_ref[qs, (kh * GQA_GROUP + g) * HEAD_DIM:(kh * GQA_GROUP + g + 1) * HEAD_DIM]
             for g in range(GQA_GROUP)], axis=0)
        kwin = kall_ref[b * CHUNK:(b + 3) * CHUNK, kh * HEAD_DIM:(kh + 1) * HEAD_DIM]
        return lax.dot_general(q, kwin, (((1,), (1,)), ((), ())),
                               preferred_element_type=jnp.float32)

    iters = [(b, kh) for b in range(nblk) for kh in range(n_kv)]
    sc_next = scores(*iters[0])
    for it, (b, kh) in enumerate(iters):
        pthr = prev_thr if b == 0 else 0
        nthr = next_thr if b == nblk - 1 else 0
        qs = slice(b * CHUNK, (b + 1) * CHUNK)
        sc = sc_next
        if it + 1 < len(iters):
            sc_next = scores(*iters[it + 1])
        if it < n_slabs - MIXER_TAIL_SLABS:
            project_slab(it)
        s_prev = jnp.where(rel >= pthr, sc[:, :CHUNK], NEG_INF)
        s_mid = sc[:, CHUNK:2 * CHUNK]
        s_next = jnp.where(rel <= nthr, sc[:, 2 * CHUNK:], NEG_INF)
        sink2 = jnp.concatenate(
            [jnp.full((CHUNK, 1), sink_ref[kh * GQA_GROUP + g] * log2e, jnp.float32)
             for g in range(GQA_GROUP)], axis=0)
        mx = jnp.maximum(
            jnp.max(jnp.maximum(jnp.maximum(s_prev, s_mid), s_next), axis=-1, keepdims=True) * c2,
            sink2)
        p_prev = jnp.exp2(s_prev * c2 - mx)
        p_mid = jnp.exp2(s_mid * c2 - mx)
        p_next = jnp.exp2(s_next * c2 - mx)
        denom = (jnp.sum(p_prev + p_mid + p_next, axis=-1, keepdims=True)
                 + jnp.exp2(sink2 - mx))
        p = jnp.concatenate([p_prev, p_mid, p_next], axis=1).astype(jnp.bfloat16)
        vwin = vall_ref[b * CHUNK:(b + 3) * CHUNK, kh * HEAD_DIM:(kh + 1) * HEAD_DIM]
        o = jnp.dot(p, vwin, preferred_element_type=jnp.float32) / denom
        for g in range(GQA_GROUP):
            hs = (kh * GQA_GROUP + g) * HEAD_DIM
            att_ref[qs, hs:hs + HEAD_DIM] = o[g * CHUNK:(g + 1) * CHUNK]
        for g in range(kh * gpi, (kh + 1) * gpi):
            gs = slice(g * CHUNK, (g + 1) * CHUNK)
            mixed = jnp.dot(ws_ref[g], vn_ref[qs, gs], preferred_element_type=jnp.float32)
            a_ref[qs, gs] = u_ref[qs, gs].astype(jnp.float32) * (mixed + bs_ref[g])
        if kh == n_kv - 1:
            m_ref[qs, :gw] = _rms(a_ref[qs, :], ga_ref[...]).astype(jnp.bfloat16)
            m_ref[qs, gw:] = _rms(att_ref[qs, :], gb_ref[...]).astype(jnp.bfloat16)
    for slab in range(n_slabs - MIXER_TAIL_SLABS, n_slabs):
        project_slab(slab)


def _mixer(h, z, sink, ws, bs, ga, gb, wo, *, seq, tm):
    t, d = h.shape
    gw = ga.shape[1]
    qw = gb.shape[1]
    kw = qw // GQA_GROUP
    cols = z.shape[1]
    kv_col_block, rem = divmod(2 * gw + qw, 2 * kw)
    assert rem == 0 and cols == 2 * gw + qw + 2 * kw
    nblk = tm // CHUNK
    nblocks = t // CHUNK
    ntiles = t // tm
    tiles_per_seq = seq // tm
    mix_tile = lambda s: jnp.minimum(s, ntiles - 1)
    mix_row = lambda s: (mix_tile(s), 0)
    proj_row = lambda s: (jnp.maximum(s - 1, 0), 0)
    const2 = lambda s: (0, 0)
    const3 = lambda s: (0, 0, 0)
    prev_blk = lambda s: (jnp.maximum(mix_tile(s) * nblk - 1, 0), kv_col_block)
    next_blk = lambda s: (jnp.minimum((mix_tile(s) + 1) * nblk, nblocks - 1), kv_col_block)
    return pl.pallas_call(
        functools.partial(_mixer_kernel, ntiles=ntiles, tiles_per_seq=tiles_per_seq, nblk=nblk,
                          gw=gw, qw=qw, kw=kw),
        grid=(ntiles + 1,),
        in_specs=[
            pl.BlockSpec(memory_space=pltpu.SMEM),
            pl.BlockSpec((tm, d), proj_row),
            pl.BlockSpec((tm, cols), mix_row),
            pl.BlockSpec((CHUNK, 2 * kw), prev_blk),
            pl.BlockSpec((CHUNK, 2 * kw), next_blk),
            pl.BlockSpec((GMLP_GROUPS, CHUNK, CHUNK), const3),
            pl.BlockSpec((GMLP_GROUPS, CHUNK, CHUNK), const3),
            pl.BlockSpec((1, gw), const2),
            pl.BlockSpec((1, qw), const2),
            pl.BlockSpec((d, d), const2, pipeline_mode=pl.Buffered(1)),
        ],
        out_specs=pl.BlockSpec((tm, d), proj_row),
        out_shape=jax.ShapeDtypeStruct((t, d), jnp.float32),
        scratch_shapes=[
            pltpu.VMEM((tm, gw), jnp.float32),
            pltpu.VMEM((tm, qw), jnp.float32),
            pltpu.VMEM((tm, d), jnp.bfloat16),
            pltpu.VMEM((tm, d), jnp.bfloat16),
            pltpu.VMEM((tm + 2 * CHUNK, kw), jnp.bfloat16),
            pltpu.VMEM((tm + 2 * CHUNK, kw), jnp.bfloat16),
        ],
        compiler_params=pltpu.CompilerParams(
            dimension_semantics=("arbitrary",),
            vmem_limit_bytes=VMEM_LIMIT_BYTES),
        name="mixer",
    )(sink, h, z, z, z, ws, bs, ga, gb, wo)


def _rope_tables(max_seq, tm):
    half = ROT_DIM // 2
    inv_freq = ROPE_THETA ** (-jnp.arange(0, ROT_DIM, 2, dtype=jnp.float32) / ROT_DIM)
    pad = jnp.zeros((HEAD_DIM - ROT_DIM,), jnp.float32)
    freq = jnp.concatenate([inv_freq, inv_freq, pad])[None, :]
    sign = jnp.concatenate([-jnp.ones((half,)), jnp.ones((half,)), pad])[None, :]
    ang_r = jnp.arange(tm, dtype=jnp.float32)[:, None] * freq
    ang_0 = jnp.arange(0, max_seq, tm, dtype=jnp.float32)[:, None] * freq
    c0, s0 = jnp.cos(ang_0), jnp.sin(ang_0)
    zero = jnp.zeros_like(c0)
    phase = jnp.stack([c0, s0, sign * c0, sign * s0] + [zero] * (PHASE_ROWS - 4), axis=1)
    return jnp.cos(ang_r), jnp.sin(ang_r), phase.reshape(-1, HEAD_DIM)


def kernel(x_prompt, x_sample, ffn1_norm, ffn1_w_gate, ffn1_w_up, ffn1_w_down, mix_norm, w_in,
           gmlp_v_norm, gmlp_w_s, gmlp_b_s, attn_sink, out_norm_gmlp, out_norm_attn, w_out,
           ffn2_norm, ffn2_w_gate, ffn2_w_up, ffn2_w_down, final_norm):
    bf = jnp.bfloat16
    assert ffn1_norm.shape[0] == 1, "one encoder layer"
    d = x_prompt.shape[-1]
    row = lambda a: a.reshape(1, -1)
    fg = row(final_norm)
    ws = gmlp_w_s[0].astype(bf)
    bs = jnp.broadcast_to(gmlp_b_s[0][:, :, None], gmlp_b_s[0].shape + (CHUNK,))
    rope = _rope_tables(max(x_prompt.shape[1], x_sample.shape[1]), INPROJ_TM)

    def flat(x):
        return x.reshape(-1, d)

    def rest_of_layer(h, seq, wg2, wu2, wd2, w_in_bf, w_out_bf):
        z = _inproj(h, row(mix_norm[0]), w_in_bf, row(gmlp_v_norm[0]), rope, seq=seq, tm=INPROJ_TM)
        h = _mixer(h, z, attn_sink[0], ws, bs, row(out_norm_gmlp[0]),
                   row(out_norm_attn[0]), w_out_bf, seq=seq, tm=MIXER_TM)
        h, _ = _ffn(h, row(ffn2_norm[0]), wg2, wu2, wd2, fg, final_norm=True, tm=FFN_TM, tf=FFN_TF)
        return h

    wg1, wu1, wd1 = (w[0].astype(bf) for w in (ffn1_w_gate, ffn1_w_up, ffn1_w_down))
    later = (ffn2_w_gate[0], ffn2_w_up[0], ffn2_w_down[0], w_in[0], w_out[0])
    h_p, later_bf = _ffn(flat(x_prompt), row(ffn1_norm[0]), wg1, wu1, wd1, fg, final_norm=False,
                         tm=FFN_TM, tf=FFN_TF, cast=later)
    h_s, _ = _ffn(flat(x_sample), row(ffn1_norm[0]), wg1, wu1, wd1, fg, final_norm=False,
                  tm=FFN_TM, tf=FFN_TF)
    y_p = rest_of_layer(h_p, x_prompt.shape[1], *later_bf)
    y_s = rest_of_layer(h_s, x_sample.shape[1], *later_bf)
    return (y_p.reshape(x_prompt.shape), y_s.reshape(x_sample.shape))
```

```python
import functools

import jax
import jax.numpy as jnp
from jax import lax
from jax.experimental import pallas as pl
from jax.experimental.pallas import tpu as pltpu

HEAD_DIM = 128
GQA_GROUP = 4
GMLP_GROUPS = 8
CHUNK = 128
ROT_DIM = HEAD_DIM // 4
ROPE_THETA = 500000.0
EPS = 1e-6
NEG_INF = -1e30

VMEM_LIMIT_BYTES = 60 * 1024 * 1024

FFN_TM = 1024
FFN_TF = 512
FFN_SLAB = 256
FFN_ROW_CHUNK = 256
CAST_BLOCK_LONG = 512
INPROJ_TM = 1024
PHASE_ROWS = 8
MIXER_TM = 512
MIXER_TAIL_SLABS = 2


def _rms(x, gain):
    return x * lax.rsqrt(jnp.mean(x * x, axis=-1, keepdims=True) + EPS) * gain


def _ffn_kernel(*refs, final_norm, n_cast):
    x_ref, g_ref, wg_ref, wu_ref, wd_ref, fg_ref = refs[:6]
    cast_in = refs[6:6 + n_cast]
    o_ref = refs[6 + n_cast]
    cast_out = refs[7 + n_cast:7 + 2 * n_cast]
    n_ref = refs[7 + 2 * n_cast]
    j = pl.program_id(1)

    n_chunks = x_ref.shape[0] // FFN_ROW_CHUNK

    def rows(r):
        return pl.ds(pl.multiple_of(r * FFN_ROW_CHUNK, FFN_ROW_CHUNK), FFN_ROW_CHUNK)

    @pl.when(j == 0)
    def _():
        def norm_chunk(r, carry):
            x = x_ref[rows(r), :]
            n_ref[rows(r), :] = _rms(x, g_ref[...]).astype(jnp.bfloat16)
            return carry
        lax.fori_loop(0, n_chunks, norm_chunk, 0)

    def step(base_ref):
        for src, dst in zip(cast_in, cast_out):
            dst[...] = src[...].astype(dst.dtype)

        n = n_ref[...]
        tf = wg_ref.shape[1]
        acts = []
        for c in range(0, tf, FFN_SLAB):
            gate = jnp.dot(n, wg_ref[:, c:c + FFN_SLAB], preferred_element_type=jnp.float32)
            up = jnp.dot(n, wu_ref[:, c:c + FFN_SLAB], preferred_element_type=jnp.float32)
            acts.append((0.5 * jax.nn.silu(gate) * up).astype(jnp.bfloat16))
        o_ref[...] = base_ref[...] + jnp.dot(jnp.concatenate(acts, axis=1), wd_ref[...],
                                             preferred_element_type=jnp.float32)

    @pl.when(j == 0)
    def _():
        step(x_ref)

    @pl.when(j > 0)
    def _():
        step(o_ref)

    if final_norm:
        @pl.when(j == pl.num_programs(1) - 1)
        def _():
            def norm_chunk(r, carry):
                o_ref[rows(r), :] = _rms(o_ref[rows(r), :], fg_ref[...])
                return carry
            lax.fori_loop(0, n_chunks, norm_chunk, 0)


def _cast_plan(w, n_row_steps, n_col_steps):
    r, c = w.shape
    transpose = r > c
    assert min(r, c) % n_row_steps == 0 and max(r, c) % CAST_BLOCK_LONG == 0
    assert max(r, c) // CAST_BLOCK_LONG <= n_col_steps
    br = r // n_row_steps if not transpose else CAST_BLOCK_LONG
    bc = CAST_BLOCK_LONG if not transpose else c // n_row_steps
    if transpose:
        nlong = r // br
        return (br, bc), (lambda i, j: (jnp.minimum(j, nlong - 1), i))
    nlong = c // bc
    return (br, bc), (lambda i, j: (i, jnp.minimum(j, nlong - 1)))


def _ffn(x, gain, wg, wu, wd, fgain, *, final_norm, tm, tf, cast=()):
    t, d = x.shape
    dff = wg.shape[1]
    grid = (t // tm, dff // tf)
    plans = [_cast_plan(w, *grid) for w in cast]
    cast_specs = [pl.BlockSpec(blk, imap) for blk, imap in plans]
    outs = pl.pallas_call(
        functools.partial(_ffn_kernel, final_norm=final_norm, n_cast=len(cast)),
        grid=grid,
        in_specs=[
            pl.BlockSpec((tm, d), lambda i, j: (i, 0)),
            pl.BlockSpec((1, d), lambda i, j: (0, 0)),
            pl.BlockSpec((d, tf), lambda i, j: (0, j)),
            pl.BlockSpec((d, tf), lambda i, j: (0, j)),
            pl.BlockSpec((tf, d), lambda i, j: (j, 0)),
            pl.BlockSpec((1, d), lambda i, j: (0, 0)),
        ] + cast_specs,
        out_specs=[pl.BlockSpec((tm, d), lambda i, j: (i, 0))] + cast_specs,
        out_shape=[jax.ShapeDtypeStruct((t, d), jnp.float32)]
        + [jax.ShapeDtypeStruct(w.shape, jnp.bfloat16) for w in cast],
        scratch_shapes=[pltpu.VMEM((tm, d), jnp.bfloat16)],
        compiler_params=pltpu.CompilerParams(
            dimension_semantics=("arbitrary", "arbitrary"),
            vmem_limit_bytes=VMEM_LIMIT_BYTES),
        name=("ffn_final" if final_norm else "ffn") + ("_cast" if cast else ""),
    )(x, gain, wg, wu, wd, fgain, *cast)
    return outs[0], tuple(outs[1:])


def _rope(x, cos, sin, low_half):
    half = ROT_DIM // 2
    partner = jnp.where(low_half, pltpu.roll(x, HEAD_DIM - half, 1), pltpu.roll(x, half, 1))
    return x * cos + partner * sin


def _inproj_kernel(h_ref, g_ref, w_ref, vg_ref, cr_ref, sr_ref, ph_ref,
                   z_ref, *, gw, qw, kw):
    n = _rms(h_ref[...], g_ref[...]).astype(jnp.bfloat16)
    c1, c2, c3, c4 = gw, 2 * gw, 2 * gw + qw, 2 * gw + qw + kw
    u_ref, vn_ref, q_ref = z_ref.at[:, 0:c1], z_ref.at[:, c1:c2], z_ref.at[:, c2:c3]
    k_ref, v_ref = z_ref.at[:, c3:c4], z_ref.at[:, c4:c4 + kw]

    def proj(lo, hi):
        return jnp.dot(n, w_ref[:, lo:hi], preferred_element_type=jnp.float32)

    cr = cr_ref[...]
    sr = sr_ref[...]
    cos = ph_ref[0:1, :] * cr - ph_ref[1:2, :] * sr
    sin = ph_ref[3:4, :] * cr + ph_ref[2:3, :] * sr
    low_half = lax.broadcasted_iota(jnp.int32, cos.shape, 1) < (ROT_DIM // 2)

    zk = proj(c3, c4)
    for hd in range(kw // HEAD_DIM):
        sl = slice(hd * HEAD_DIM, (hd + 1) * HEAD_DIM)
        k_ref[:, sl] = _rope(zk[:, sl], cos, sin, low_half).astype(jnp.bfloat16)
    zq = proj(c2, c3)
    for hd in range(qw // HEAD_DIM):
        sl = slice(hd * HEAD_DIM, (hd + 1) * HEAD_DIM)
        q_ref[:, sl] = _rope(zq[:, sl], cos, sin, low_half).astype(jnp.bfloat16)
    u_ref[...] = jax.nn.gelu(proj(0, c1)).astype(jnp.bfloat16)
    vn_ref[...] = _rms(jax.nn.gelu(proj(c1, c2)), vg_ref[...]).astype(jnp.bfloat16)
    v_ref[...] = proj(c4, c4 + kw).astype(jnp.bfloat16)


def _inproj(h, gain, w_in, vgain, rope, *, seq, tm):
    t, d = h.shape
    gw = vgain.shape[1]
    qw = d - gw
    kw = qw // GQA_GROUP
    cols = w_in.shape[1]
    tiles_per_seq = seq // tm
    cr, sr, phase = rope
    row = lambda i: (i, 0)
    const = lambda i: (0, 0)
    return pl.pallas_call(
        functools.partial(_inproj_kernel, gw=gw, qw=qw, kw=kw),
        grid=(t // tm,),
        in_specs=[
            pl.BlockSpec((tm, d), row),
            pl.BlockSpec((1, d), const),
            pl.BlockSpec((d, cols), const, pipeline_mode=pl.Buffered(1)),
            pl.BlockSpec((1, gw), const),
            pl.BlockSpec((tm, HEAD_DIM), const),
            pl.BlockSpec((tm, HEAD_DIM), const),
            pl.BlockSpec((PHASE_ROWS, HEAD_DIM), lambda i: (i % tiles_per_seq, 0)),
        ],
        out_specs=pl.BlockSpec((tm, cols), row),
        out_shape=jax.ShapeDtypeStruct((t, cols), jnp.bfloat16),
        compiler_params=pltpu.CompilerParams(
            dimension_semantics=("arbitrary",),
            vmem_limit_bytes=VMEM_LIMIT_BYTES),
        name="inproj",
    )(h, gain, w_in, vgain, cr, sr, phase)


def _mixer_kernel(sink_ref, h_ref, z_ref, zp_ref, zn_ref, ws_ref, bs_ref,
                  ga_ref, gb_ref, wo_ref, o_ref,
                  a_ref, att_ref, m_ref, mprev_ref, kall_ref, vall_ref,
                  *, ntiles, tiles_per_seq, nblk, gw, qw, kw):
    s = pl.program_id(0)
    u_ref, vn_ref = z_ref.at[:, 0:gw], z_ref.at[:, gw:2 * gw]
    q_ref = z_ref.at[:, 2 * gw:2 * gw + qw]
    k_ref, v_ref = z_ref.at[:, 2 * gw + qw:2 * gw + qw + kw], z_ref.at[:, 2 * gw + qw + kw:]
    kp_ref, vp_ref = zp_ref.at[:, 0:kw], zp_ref.at[:, kw:]
    kn_ref, vn2_ref = zn_ref.at[:, 0:kw], zn_ref.at[:, kw:]
    n_kv = kw // HEAD_DIM
    rows = GQA_GROUP * CHUNK

    @pl.when(s == 0)
    def _():
        m_ref[...] = jnp.zeros_like(m_ref)

    mprev_ref[...] = m_ref[...]

    kall_ref[0:CHUNK] = kp_ref[...]
    kall_ref[CHUNK:(nblk + 1) * CHUNK] = k_ref[...]
    kall_ref[(nblk + 1) * CHUNK:] = kn_ref[...]
    vall_ref[0:CHUNK] = vp_ref[...]
    vall_ref[CHUNK:(nblk + 1) * CHUNK] = v_ref[...]
    vall_ref[(nblk + 1) * CHUNK:] = vn2_ref[...]

    t_in_seq = jnp.minimum(s, ntiles - 1) % tiles_per_seq
    prev_thr = jnp.where(t_in_seq == 0, 2 * CHUNK, 0)
    next_thr = jnp.where(t_in_seq == tiles_per_seq - 1, -2 * CHUNK, 0)
    rel = (lax.broadcasted_iota(jnp.int32, (rows, CHUNK), 1)
           - (lax.broadcasted_iota(jnp.int32, (rows, CHUNK), 0) & (CHUNK - 1)))
    log2e = 1.4426950408889634
    c2 = (HEAD_DIM ** -0.5) * log2e
    n_slabs = nblk * n_kv
    pw = o_ref.shape[1] // n_slabs
    gpi = GMLP_GROUPS // n_kv

    def project_slab(slab):
        ps = slice(slab * pw, (slab + 1) * pw)
        o_ref[:, ps] = h_ref[:, ps] + jnp.dot(mprev_ref[...], wo_ref[:, ps],
                                              preferred_element_type=jnp.float32)

    def scores(b, kh):
        qs = slice(b * CHUNK, (b + 1) * CHUNK)
        q = jnp.concatenate(
            [q_ref[qs, (kh * GQA_GROUP + g) * HEAD_DIM:(kh * GQA_GROUP + g + 1) * HEAD_DIM]
             for g in range(GQA_GROUP)], axis=0)
        kwin = kall_ref[b * CHUNK:(b + 3) * CHUNK, kh * HEAD_DIM:(kh + 1) * HEAD_DIM]
        return lax.dot_general(q, kwin, (((1,), (1,)), ((), ())),
                               preferred_element_type=jnp.float32)

    iters = [(b, kh) for b in range(nblk) for kh in range(n_kv)]
    sc_next = scores(*iters[0])
    for it, (b, kh) in enumerate(iters):
        pthr = prev_thr if b == 0 else 0
        nthr = next_thr if b == nblk - 1 else 0
        qs = slice(b * CHUNK, (b + 1) * CHUNK)
        sc = sc_next
        if it + 1 < len(iters):
            sc_next = scores(*iters[it + 1])
        if it < n_slabs - MIXER_TAIL_SLABS:
            project_slab(it)
        s_prev = jnp.where(rel >= pthr, sc[:, :CHUNK], NEG_INF)
        s_mid = sc[:, CHUNK:2 * CHUNK]
        s_next = jnp.where(rel <= nthr, sc[:, 2 * CHUNK:], NEG_INF)
        sink2 = jnp.concatenate(
            [jnp.full((CHUNK, 1), sink_ref[kh * GQA_GROUP + g] * log2e, jnp.float32)
             for g in range(GQA_GROUP)], axis=0)
        mx = jnp.maximum(
            jnp.max(jnp.maximum(jnp.maximum(s_prev, s_mid), s_next), axis=-1, keepdims=True) * c2,
            sink2)
        p_prev = jnp.exp2(s_prev * c2 - mx)
        p_mid = jnp.exp2(s_mid * c2 - mx)
        p_next = jnp.exp2(s_next * c2 - mx)
        denom = (jnp.sum(p_prev + p_mid + p_next, axis=-1, keepdims=True)
                 + jnp.exp2(sink2 - mx))
        p = jnp.concatenate([p_prev, p_mid, p_next], axis=1).astype(jnp.bfloat16)
        vwin = vall_ref[b * CHUNK:(b + 3) * CHUNK, kh * HEAD_DIM:(kh + 1) * HEAD_DIM]
        o = jnp.dot(p, vwin, preferred_element_type=jnp.float32) / denom
        for g in range(GQA_GROUP):
            hs = (kh * GQA_GROUP + g) * HEAD_DIM
            att_ref[qs, hs:hs + HEAD_DIM] = o[g * CHUNK:(g + 1) * CHUNK]
        for g in range(kh * gpi, (kh + 1) * gpi):
            gs = slice(g * CHUNK, (g + 1) * CHUNK)
            mixed = jnp.dot(ws_ref[g], vn_ref[qs, gs], preferred_element_type=jnp.float32)
            a_ref[qs, gs] = u_ref[qs, gs].astype(jnp.float32) * (mixed + bs_ref[g])
        if kh == n_kv - 1:
            m_ref[qs, :gw] = _rms(a_ref[qs, :], ga_ref[...]).astype(jnp.bfloat16)
            m_ref[qs, gw:] = _rms(att_ref[qs, :], gb_ref[...]).astype(jnp.bfloat16)
    for slab in range(n_slabs - MIXER_TAIL_SLABS, n_slabs):
        project_slab(slab)


def _mixer(h, z, sink, ws, bs, ga, gb, wo, *, seq, tm):
    t, d = h.shape
    gw = ga.shape[1]
    qw = gb.shape[1]
    kw = qw // GQA_GROUP
    cols = z.shape[1]
    kv_col_block, rem = divmod(2 * gw + qw, 2 * kw)
    assert rem == 0 and cols == 2 * gw + qw + 2 * kw
    nblk = tm // CHUNK
    nblocks = t // CHUNK
    ntiles = t // tm
    tiles_per_seq = seq // tm
    mix_tile = lambda s: jnp.minimum(s, ntiles - 1)
    mix_row = lambda s: (mix_tile(s), 0)
    proj_row = lambda s: (jnp.maximum(s - 1, 0), 0)
    const2 = lambda s: (0, 0)
    const3 = lambda s: (0, 0, 0)
    prev_blk = lambda s: (jnp.maximum(mix_tile(s) * nblk - 1, 0), kv_col_block)
    next_blk = lambda s: (jnp.minimum((mix_tile(s) + 1) * nblk, nblocks - 1), kv_col_block)
    return pl.pallas_call(
        functools.partial(_mixer_kernel, ntiles=ntiles, tiles_per_seq=tiles_per_seq, nblk=nblk,
                          gw=gw, qw=qw, kw=kw),
        grid=(ntiles + 1,),
        in_specs=[
            pl.BlockSpec(memory_space=pltpu.SMEM),
            pl.BlockSpec((tm, d), proj_row),
            pl.BlockSpec((tm, cols), mix_row),
            pl.BlockSpec((CHUNK, 2 * kw), prev_blk),
            pl.BlockSpec((CHUNK, 2 * kw), next_blk),
            pl.BlockSpec((GMLP_GROUPS, CHUNK, CHUNK), const3),
            pl.BlockSpec((GMLP_GROUPS, CHUNK, CHUNK), const3),
            pl.BlockSpec((1, gw), const2),
            pl.BlockSpec((1, qw), const2),
            pl.BlockSpec((d, d), const2, pipeline_mode=pl.Buffered(1)),
        ],
        out_specs=pl.BlockSpec((tm, d), proj_row),
        out_shape=jax.ShapeDtypeStruct((t, d), jnp.float32),
        scratch_shapes=[
            pltpu.VMEM((tm, gw), jnp.float32),
            pltpu.VMEM((tm, qw), jnp.float32),
            pltpu.VMEM((tm, d), jnp.bfloat16),
            pltpu.VMEM((tm, d), jnp.bfloat16),
            pltpu.VMEM((tm + 2 * CHUNK, kw), jnp.bfloat16),
            pltpu.VMEM((tm + 2 * CHUNK, kw), jnp.bfloat16),
        ],
        compiler_params=pltpu.CompilerParams(
            dimension_semantics=("arbitrary",),
            vmem_limit_bytes=VMEM_LIMIT_BYTES),
        name="mixer",
    )(sink, h, z, z, z, ws, bs, ga, gb, wo)


def _rope_tables(max_seq, tm):
    half = ROT_DIM // 2
    inv_freq = ROPE_THETA ** (-jnp.arange(0, ROT_DIM, 2, dtype=jnp.float32) / ROT_DIM)
    pad = jnp.zeros((HEAD_DIM - ROT_DIM,), jnp.float32)
    freq = jnp.concatenate([inv_freq, inv_freq, pad])[None, :]
    sign = jnp.concatenate([-jnp.ones((half,)), jnp.ones((half,)), pad])[None, :]
    ang_r = jnp.arange(tm, dtype=jnp.float32)[:, None] * freq
    ang_0 = jnp.arange(0, max_seq, tm, dtype=jnp.float32)[:, None] * freq
    c0, s0 = jnp.cos(ang_0), jnp.sin(ang_0)
    zero = jnp.zeros_like(c0)
    phase = jnp.stack([c0, s0, sign * c0, sign * s0] + [zero] * (PHASE_ROWS - 4), axis=1)
    return jnp.cos(ang_r), jnp.sin(ang_r), phase.reshape(-1, HEAD_DIM)


def kernel(x_prompt, x_sample, ffn1_norm, ffn1_w_gate, ffn1_w_up, ffn1_w_down, mix_norm, w_in,
           gmlp_v_norm, gmlp_w_s, gmlp_b_s, attn_sink, out_norm_gmlp, out_norm_attn, w_out,
           ffn2_norm, ffn2_w_gate, ffn2_w_up, ffn2_w_down, final_norm):
    bf = jnp.bfloat16
    assert ffn1_norm.shape[0] == 1, "one encoder layer"
    d = x_prompt.shape[-1]
    row = lambda a: a.reshape(1, -1)
    fg = row(final_norm)
    ws = gmlp_w_s[0].astype(bf)
    bs = jnp.broadcast_to(gmlp_b_s[0][:, :, None], gmlp_b_s[0].shape + (CHUNK,))
    rope = _rope_tables(max(x_prompt.shape[1], x_sample.shape[1]), INPROJ_TM)

    def flat(x):
        return x.reshape(-1, d)

    def rest_of_layer(h, seq, wg2, wu2, wd2, w_in_bf, w_out_bf):
        z = _inproj(h, row(mix_norm[0]), w_in_bf, row(gmlp_v_norm[0]), rope, seq=seq, tm=INPROJ_TM)
        h = _mixer(h, z, attn_sink[0], ws, bs, row(out_norm_gmlp[0]),
                   row(out_norm_attn[0]), w_out_bf, seq=seq, tm=MIXER_TM)
        h, _ = _ffn(h, row(ffn2_norm[0]), wg2, wu2, wd2, fg, final_norm=True, tm=FFN_TM, tf=FFN_TF)
        return h

    wg1, wu1, wd1 = (w[0].astype(bf) for w in (ffn1_w_gate, ffn1_w_up, ffn1_w_down))
    later = (ffn2_w_gate[0], ffn2_w_up[0], ffn2_w_down[0], w_in[0], w_out[0])
    h_p, later_bf = _ffn(flat(x_prompt), row(ffn1_norm[0]), wg1, wu1, wd1, fg, final_norm=False,
                         tm=FFN_TM, tf=FFN_TF, cast=later)
    h_s, _ = _ffn(flat(x_sample), row(ffn1_norm[0]), wg1, wu1, wd1, fg, final_norm=False,
                  tm=FFN_TM, tf=FFN_TF)
    y_p = rest_of_layer(h_p, x_prompt.shape[1], *later_bf)
    y_s = rest_of_layer(h_s, x_sample.shape[1], *later_bf)
    return (y_p.reshape(x_prompt.shape), y_s.reshape(x_sample.shape))
```

```python
import functools

import jax
import jax.numpy as jnp
from jax import lax
from jax.experimental import pallas as pl
from jax.experimental.pallas import tpu as pltpu

HEAD_DIM = 128
GQA_GROUP = 4
GMLP_GROUPS = 8
CHUNK = 128
ROT_DIM = HEAD_DIM // 4
ROPE_THETA = 500000.0
EPS = 1e-6
NEG_INF = -1e30

VMEM_LIMIT_BYTES = 60 * 1024 * 1024

FFN_TM = 1024
FFN_TF = 512
FFN_SLAB = 256
FFN_ROW_CHUNK = 256
CAST_BLOCK_LONG = 512
INPROJ_TM = 1024
PHASE_ROWS = 8
MIXER_TM = 512
MIXER_TAIL_SLABS = 2


def _rms(x, gain):
    return x * lax.rsqrt(jnp.mean(x * x, axis=-1, keepdims=True) + EPS) * gain


def _ffn_kernel(*refs, final_norm, n_cast):
    x_ref, g_ref, wg_ref, wu_ref, wd_ref, fg_ref = refs[:6]
    cast_in = refs[6:6 + n_cast]
    o_ref = refs[6 + n_cast]
    cast_out = refs[7 + n_cast:7 + 2 * n_cast]
    n_ref = refs[7 + 2 * n_cast]
    j = pl.program_id(1)

    n_chunks = x_ref.shape[0] // FFN_ROW_CHUNK

    def rows(r):
        return pl.ds(pl.multiple_of(r * FFN_ROW_CHUNK, FFN_ROW_CHUNK), FFN_ROW_CHUNK)

    def step(base_ref):
        for src, dst in zip(cast_in, cast_out):
            dst[...] = src[...].astype(dst.dtype)

        n = n_ref[...]
        tf = wg_ref.shape[1]
        acts = []
        for c in range(0, tf, FFN_SLAB):
            gate = jnp.dot(n, wg_ref[:, c:c + FFN_SLAB], preferred_element_type=jnp.float32)
            up = jnp.dot(n, wu_ref[:, c:c + FFN_SLAB], preferred_element_type=jnp.float32)
            acts.append((0.5 * jax.nn.silu(gate) * up).astype(jnp.bfloat16))
        o_ref[...] = base_ref[...] + jnp.dot(jnp.concatenate(acts, axis=1), wd_ref[...],
                                             preferred_element_type=jnp.float32)

    @pl.when(j == 0)
    def _():
        for r in range(0, x_ref.shape[0], FFN_ROW_CHUNK):
            n_ref[r:r + FFN_ROW_CHUNK, :] = _rms(x_ref[r:r + FFN_ROW_CHUNK, :],
                                                 g_ref[...]).astype(jnp.bfloat16)
        step(x_ref)

    @pl.when(j > 0)
    def _():
        step(o_ref)

    if final_norm:
        @pl.when(j == pl.num_programs(1) - 1)
        def _():
            def norm_chunk(r, carry):
                o_ref[rows(r), :] = _rms(o_ref[rows(r), :], fg_ref[...])
                return carry
            lax.fori_loop(0, n_chunks, norm_chunk, 0)


def _cast_plan(w, n_row_steps, n_col_steps):
    r, c = w.shape
    transpose = r > c
    assert min(r, c) % n_row_steps == 0 and max(r, c) % CAST_BLOCK_LONG == 0
    assert max(r, c) // CAST_BLOCK_LONG <= n_col_steps
    br = r // n_row_steps if not transpose else CAST_BLOCK_LONG
    bc = CAST_BLOCK_LONG if not transpose else c // n_row_steps
    if transpose:
        nlong = r // br
        return (br, bc), (lambda i, j: (jnp.minimum(j, nlong - 1), i))
    nlong = c // bc
    return (br, bc), (lambda i, j: (i, jnp.minimum(j, nlong - 1)))


def _ffn(x, gain, wg, wu, wd, fgain, *, final_norm, tm, tf, cast=()):
    t, d = x.shape
    dff = wg.shape[1]
    grid = (t // tm, dff // tf)
    plans = [_cast_plan(w, *grid) for w in cast]
    cast_specs = [pl.BlockSpec(blk, imap) for blk, imap in plans]
    outs = pl.pallas_call(
        functools.partial(_ffn_kernel, final_norm=final_norm, n_cast=len(cast)),
        grid=grid,
        in_specs=[
            pl.BlockSpec((tm, d), lambda i, j: (i, 0)),
            pl.BlockSpec((1, d), lambda i, j: (0, 0)),
            pl.BlockSpec((d, tf), lambda i, j: (0, j)),
            pl.BlockSpec((d, tf), lambda i, j: (0, j)),
            pl.BlockSpec((tf, d), lambda i, j: (j, 0)),
            pl.BlockSpec((1, d), lambda i, j: (0, 0)),
        ] + cast_specs,
        out_specs=[pl.BlockSpec((tm, d), lambda i, j: (i, 0))] + cast_specs,
        out_shape=[jax.ShapeDtypeStruct((t, d), jnp.float32)]
        + [jax.ShapeDtypeStruct(w.shape, jnp.bfloat16) for w in cast],
        scratch_shapes=[pltpu.VMEM((tm, d), jnp.bfloat16)],
        compiler_params=pltpu.CompilerParams(
            dimension_semantics=("arbitrary", "arbitrary"),
            vmem_limit_bytes=VMEM_LIMIT_BYTES),
        name=("ffn_final" if final_norm else "ffn") + ("_cast" if cast else ""),
    )(x, gain, wg, wu, wd, fgain, *cast)
    return outs[0], tuple(outs[1:])


def _rope(x, cos, sin, low_half):
    half = ROT_DIM // 2
    partner = jnp.where(low_half, pltpu.roll(x, HEAD_DIM - half, 1), pltpu.roll(x, half, 1))
    return x * cos + partner * sin


def _inproj_kernel(h_ref, g_ref, w_ref, vg_ref, cr_ref, sr_ref, ph_ref,
                   z_ref, *, gw, qw, kw):
    n = _rms(h_ref[...], g_ref[...]).astype(jnp.bfloat16)
    c1, c2, c3, c4 = gw, 2 * gw, 2 * gw + qw, 2 * gw + qw + kw
    u_ref, vn_ref, q_ref = z_ref.at[:, 0:c1], z_ref.at[:, c1:c2], z_ref.at[:, c2:c3]
    k_ref, v_ref = z_ref.at[:, c3:c4], z_ref.at[:, c4:c4 + kw]

    def proj(lo, hi):
        return jnp.dot(n, w_ref[:, lo:hi], preferred_element_type=jnp.float32)

    cr = cr_ref[...]
    sr = sr_ref[...]
    cos = ph_ref[0:1, :] * cr - ph_ref[1:2, :] * sr
    sin = ph_ref[3:4, :] * cr + ph_ref[2:3, :] * sr
    low_half = lax.broadcasted_iota(jnp.int32, cos.shape, 1) < (ROT_DIM // 2)

    zk = proj(c3, c4)
    for hd in range(kw // HEAD_DIM):
        sl = slice(hd * HEAD_DIM, (hd + 1) * HEAD_DIM)
        k_ref[:, sl] = _rope(zk[:, sl], cos, sin, low_half).astype(jnp.bfloat16)
    zq = proj(c2, c3)
    for hd in range(qw // HEAD_DIM):
        sl = slice(hd * HEAD_DIM, (hd + 1) * HEAD_DIM)
        q_ref[:, sl] = _rope(zq[:, sl], cos, sin, low_half).astype(jnp.bfloat16)
    u_ref[...] = jax.nn.gelu(proj(0, c1)).astype(jnp.bfloat16)
    vn_ref[...] = _rms(jax.nn.gelu(proj(c1, c2)), vg_ref[...]).astype(jnp.bfloat16)
    v_ref[...] = proj(c4, c4 + kw).astype(jnp.bfloat16)


def _inproj(h, gain, w_in, vgain, rope, *, seq, tm):
    t, d = h.shape
    gw = vgain.shape[1]
    qw = d - gw
    kw = qw // GQA_GROUP
    cols = w_in.shape[1]
    tiles_per_seq = seq // tm
    cr, sr, phase = rope
    row = lambda i: (i, 0)
    const = lambda i: (0, 0)
    return pl.pallas_call(
        functools.partial(_inproj_kernel, gw=gw, qw=qw, kw=kw),
        grid=(t // tm,),
        in_specs=[
            pl.BlockSpec((tm, d), row),
            pl.BlockSpec((1, d), const),
            pl.BlockSpec((d, cols), const, pipeline_mode=pl.Buffered(1)),
            pl.BlockSpec((1, gw), const),
            pl.BlockSpec((tm, HEAD_DIM), const),
            pl.BlockSpec((tm, HEAD_DIM), const),
            pl.BlockSpec((PHASE_ROWS, HEAD_DIM), lambda i: (i % tiles_per_seq, 0)),
        ],
        out_specs=pl.BlockSpec((tm, cols), row),
        out_shape=jax.ShapeDtypeStruct((t, cols), jnp.bfloat16),
        compiler_params=pltpu.CompilerParams(
            dimension_semantics=("arbitrary",),
            vmem_limit_bytes=VMEM_LIMIT_BYTES),
        name="inproj",
    )(h, gain, w_in, vgain, cr, sr, phase)


def _mixer_kernel(sink_ref, h_ref, z_ref, zp_ref, zn_ref, ws_ref, bs_ref,
                  ga_ref, gb_ref, wo_ref, o_ref,
                  a_ref, att_ref, m_ref, mprev_ref, kall_ref, vall_ref,
                  *, ntiles, tiles_per_seq, nblk, gw, qw, kw):
    s = pl.program_id(0)
    u_ref, vn_ref = z_ref.at[:, 0:gw], z_ref.at[:, gw:2 * gw]
    q_ref = z_ref.at[:, 2 * gw:2 * gw + qw]
    k_ref, v_ref = z_ref.at[:, 2 * gw + qw:2 * gw + qw + kw], z_ref.at[:, 2 * gw + qw + kw:]
    kp_ref, vp_ref = zp_ref.at[:, 0:kw], zp_ref.at[:, kw:]
    kn_ref, vn2_ref = zn_ref.at[:, 0:kw], zn_ref.at[:, kw:]
    n_kv = kw // HEAD_DIM
    rows = GQA_GROUP * CHUNK

    @pl.when(s == 0)
    def _():
        m_ref[...] = jnp.zeros_like(m_ref)

    mprev_ref[...] = m_ref[...]

    kall_ref[0:CHUNK] = kp_ref[...]
    kall_ref[CHUNK:(nblk + 1) * CHUNK] = k_ref[...]
    kall_ref[(nblk + 1) * CHUNK:] = kn_ref[...]
    vall_ref[0:CHUNK] = vp_ref[...]
    vall_ref[CHUNK:(nblk + 1) * CHUNK] = v_ref[...]
    vall_ref[(nblk + 1) * CHUNK:] = vn2_ref[...]

    t_in_seq = jnp.minimum(s, ntiles - 1) % tiles_per_seq
    prev_thr = jnp.where(t_in_seq == 0, 2 * CHUNK, 0)
    next_thr = jnp.where(t_in_seq == tiles_per_seq - 1, -2 * CHUNK, 0)
    rel = (lax.broadcasted_iota(jnp.int32, (rows, CHUNK), 1)
           - (lax.broadcasted_iota(jnp.int32, (rows, CHUNK), 0) & (CHUNK - 1)))
    log2e = 1.4426950408889634
    c2 = (HEAD_DIM ** -0.5) * log2e
    n_slabs = nblk * n_kv
    pw = o_ref.shape[1] // n_slabs
    gpi = GMLP_GROUPS // n_kv

    def project_slab(slab):
        ps = slice(slab * pw, (slab + 1) * pw)
        o_ref[:, ps] = h_ref[:, ps] + jnp.dot(mprev_ref[...], wo_ref[:, ps],
                                              preferred_element_type=jnp.float32)

    def scores(b, kh):
        qs = slice(b * CHUNK, (b + 1) * CHUNK)
        q = jnp.concatenate(
            [q_ref[qs, (kh * GQA_GROUP + g) * HEAD_DIM:(kh * GQA_GROUP + g + 1) * HEAD_DIM]
             for g in range(GQA_GROUP)], axis=0)
        kwin = kall_ref[b * CHUNK:(b + 3) * CHUNK, kh * HEAD_DIM:(kh + 1) * HEAD_DIM]
        return lax.dot_general(q, kwin, (((1,), (1,)), ((), ())),
                               preferred_element_type=jnp.float32)

    iters = [(b, kh) for b in range(nblk) for kh in range(n_kv)]
    sc_next = scores(*iters[0])
    for it, (b, kh) in enumerate(iters):
        pthr = prev_thr if b == 0 else 0
        nthr = next_thr if b == nblk - 1 else 0
        qs = slice(b * CHUNK, (b + 1) * CHUNK)
        sc = sc_next
        if it + 1 < len(iters):
            sc_next = scores(*iters[it + 1])
        if it < n_slabs - MIXER_TAIL_SLABS:
            project_slab(it)
        s_prev = jnp.where(rel >= pthr, sc[:, :CHUNK], NEG_INF)
        s_mid = sc[:, CHUNK:2 * CHUNK]
        s_next = jnp.where(rel <= nthr, sc[:, 2 * CHUNK:], NEG_INF)
        sink2 = jnp.concatenate(
            [jnp.full((CHUNK, 1), sink_ref[kh * GQA_GROUP + g] * log2e, jnp.float32)
             for g in range(GQA_GROUP)], axis=0)
        mx = jnp.maximum(
            jnp.max(jnp.maximum(jnp.maximum(s_prev, s_mid), s_next), axis=-1, keepdims=True) * c2,
            sink2)
        p_prev = jnp.exp2(s_prev * c2 - mx)
        p_mid = jnp.exp2(s_mid * c2 - mx)
        p_next = jnp.exp2(s_next * c2 - mx)
        denom = (jnp.sum(p_prev + p_mid + p_next, axis=-1, keepdims=True)
                 + jnp.exp2(sink2 - mx))
        p = jnp.concatenate([p_prev, p_mid, p_next], axis=1).astype(jnp.bfloat16)
        vwin = vall_ref[b * CHUNK:(b + 3) * CHUNK, kh * HEAD_DIM:(kh + 1) * HEAD_DIM]
        o = jnp.dot(p, vwin, preferred_element_type=jnp.float32) / denom
        for g in range(GQA_GROUP):
            hs = (kh * GQA_GROUP + g) * HEAD_DIM
            att_ref[qs, hs:hs + HEAD_DIM] = o[g * CHUNK:(g + 1) * CHUNK]
        for g in range(kh * gpi, (kh + 1) * gpi):
            gs = slice(g * CHUNK, (g + 1) * CHUNK)
            mixed = jnp.dot(ws_ref[g], vn_ref[qs, gs], preferred_element_type=jnp.float32)
            a_ref[qs, gs] = u_ref[qs, gs].astype(jnp.float32) * (mixed + bs_ref[g])
        if kh == n_kv - 1:
            m_ref[qs, :gw] = _rms(a_ref[qs, :], ga_ref[...]).astype(jnp.bfloat16)
            m_ref[qs, gw:] = _rms(att_ref[qs, :], gb_ref[...]).astype(jnp.bfloat16)
    for slab in range(n_slabs - MIXER_TAIL_SLABS, n_slabs):
        project_slab(slab)


def _mixer(h, z, sink, ws, bs, ga, gb, wo, *, seq, tm):
    t, d = h.shape
    gw = ga.shape[1]
    qw = gb.shape[1]
    kw = qw // GQA_GROUP
    cols = z.shape[1]
    kv_col_block, rem = divmod(2 * gw + qw, 2 * kw)
    assert rem == 0 and cols == 2 * gw + qw + 2 * kw
    nblk = tm // CHUNK
    nblocks = t // CHUNK
    ntiles = t // tm
    tiles_per_seq = seq // tm
    mix_tile = lambda s: jnp.minimum(s, ntiles - 1)
    mix_row = lambda s: (mix_tile(s), 0)
    proj_row = lambda s: (jnp.maximum(s - 1, 0), 0)
    const2 = lambda s: (0, 0)
    const3 = lambda s: (0, 0, 0)
    prev_blk = lambda s: (jnp.maximum(mix_tile(s) * nblk - 1, 0), kv_col_block)
    next_blk = lambda s: (jnp.minimum((mix_tile(s) + 1) * nblk, nblocks - 1), kv_col_block)
    return pl.pallas_call(
        functools.partial(_mixer_kernel, ntiles=ntiles, tiles_per_seq=tiles_per_seq, nblk=nblk,
                          gw=gw, qw=qw, kw=kw),
        grid=(ntiles + 1,),
        in_specs=[
            pl.BlockSpec(memory_space=pltpu.SMEM),
            pl.BlockSpec((tm, d), proj_row),
            pl.BlockSpec((tm, cols), mix_row),
            pl.BlockSpec((CHUNK, 2 * kw), prev_blk),
            pl.BlockSpec((CHUNK, 2 * kw), next_blk),
            pl.BlockSpec((GMLP_GROUPS, CHUNK, CHUNK), const3),
            pl.BlockSpec((GMLP_GROUPS, CHUNK, CHUNK), const3),
            pl.BlockSpec((1, gw), const2),
            pl.BlockSpec((1, qw), const2),
            pl.BlockSpec((d, d), const2, pipeline_mode=pl.Buffered(1)),
        ],
        out_specs=pl.BlockSpec((tm, d), proj_row),
        out_shape=jax.ShapeDtypeStruct((t, d), jnp.float32),
        scratch_shapes=[
            pltpu.VMEM((tm, gw), jnp.float32),
            pltpu.VMEM((tm, qw), jnp.float32),
            pltpu.VMEM((tm, d), jnp.bfloat16),
            pltpu.VMEM((tm, d), jnp.bfloat16),
            pltpu.VMEM((tm + 2 * CHUNK, kw), jnp.bfloat16),
            pltpu.VMEM((tm + 2 * CHUNK, kw), jnp.bfloat16),
        ],
        compiler_params=pltpu.CompilerParams(
            dimension_semantics=("arbitrary",),
            vmem_limit_bytes=VMEM_LIMIT_BYTES),
        name="mixer",
    )(sink, h, z, z, z, ws, bs, ga, gb, wo)


def _rope_tables(max_seq, tm):
    half = ROT_DIM // 2
    inv_freq = ROPE_THETA ** (-jnp.arange(0, ROT_DIM, 2, dtype=jnp.float32) / ROT_DIM)
    pad = jnp.zeros((HEAD_DIM - ROT_DIM,), jnp.float32)
    freq = jnp.concatenate([inv_freq, inv_freq, pad])[None, :]
    sign = jnp.concatenate([-jnp.ones((half,)), jnp.ones((half,)), pad])[None, :]
    ang_r = jnp.arange(tm, dtype=jnp.float32)[:, None] * freq
    ang_0 = jnp.arange(0, max_seq, tm, dtype=jnp.float32)[:, None] * freq
    c0, s0 = jnp.cos(ang_0), jnp.sin(ang_0)
    zero = jnp.zeros_like(c0)
    phase = jnp.stack([c0, s0, sign * c0, sign * s0] + [zero] * (PHASE_ROWS - 4), axis=1)
    return jnp.cos(ang_r), jnp.sin(ang_r), phase.reshape(-1, HEAD_DIM)


def kernel(x_prompt, x_sample, ffn1_norm, ffn1_w_gate, ffn1_w_up, ffn1_w_down, mix_norm, w_in,
           gmlp_v_norm, gmlp_w_s, gmlp_b_s, attn_sink, out_norm_gmlp, out_norm_attn, w_out,
           ffn2_norm, ffn2_w_gate, ffn2_w_up, ffn2_w_down, final_norm):
    bf = jnp.bfloat16
    assert ffn1_norm.shape[0] == 1, "one encoder layer"
    d = x_prompt.shape[-1]
    row = lambda a: a.reshape(1, -1)
    fg = row(final_norm)
    ws = gmlp_w_s[0].astype(bf)
    bs = jnp.broadcast_to(gmlp_b_s[0][:, :, None], gmlp_b_s[0].shape + (CHUNK,))
    rope = _rope_tables(max(x_prompt.shape[1], x_sample.shape[1]), INPROJ_TM)

    def flat(x):
        return x.reshape(-1, d)

    def rest_of_layer(h, seq, wg2, wu2, wd2, w_in_bf, w_out_bf):
        z = _inproj(h, row(mix_norm[0]), w_in_bf, row(gmlp_v_norm[0]), rope, seq=seq, tm=INPROJ_TM)
        h = _mixer(h, z, attn_sink[0], ws, bs, row(out_norm_gmlp[0]),
                   row(out_norm_attn[0]), w_out_bf, seq=seq, tm=MIXER_TM)
        h, _ = _ffn(h, row(ffn2_norm[0]), wg2, wu2, wd2, fg, final_norm=True, tm=FFN_TM, tf=FFN_TF)
        return h

    wg1, wu1, wd1 = (w[0].astype(bf) for w in (ffn1_w_gate, ffn1_w_up, ffn1_w_down))
    later = (ffn2_w_gate[0], ffn2_w_up[0], ffn2_w_down[0], w_in[0], w_out[0])
    h_p, later_bf = _ffn(flat(x_prompt), row(ffn1_norm[0]), wg1, wu1, wd1, fg, final_norm=False,
                         tm=FFN_TM, tf=FFN_TF, cast=later)
    h_s, _ = _ffn(flat(x_sample), row(ffn1_norm[0]), wg1, wu1, wd1, fg, final_norm=False,
                  tm=FFN_TM, tf=FFN_TF)
    y_p = rest_of_layer(h_p, x_prompt.shape[1], *later_bf)
    y_s = rest_of_layer(h_s, x_sample.shape[1], *later_bf)
    return (y_p.reshape(x_prompt.shape), y_s.reshape(x_sample.shape))
```

```python
import functools

import jax
import jax.numpy as jnp
from jax import lax
from jax.experimental import pallas as pl
from jax.experimental.pallas import tpu as pltpu

HEAD_DIM = 128
GQA_GROUP = 4
GMLP_GROUPS = 8
CHUNK = 128
ROT_DIM = HEAD_DIM // 4
ROPE_THETA = 500000.0
EPS = 1e-6
NEG_INF = -1e30

VMEM_LIMIT_BYTES = 60 * 1024 * 1024

FFN_TM = 1024
FFN_TF = 512
FFN_SLAB = 256
FFN_ROW_CHUNK = 256
CAST_BLOCK_LONG = 512
INPROJ_TM = 1024
PHASE_ROWS = 8
MIXER_TM = 512
MIXER_TAIL_SLABS = 2


def _rms(x, gain):
    return x * lax.rsqrt(jnp.mean(x * x, axis=-1, keepdims=True) + EPS) * gain


def _ffn_kernel(*refs, final_norm, n_cast):
    x_ref, g_ref, wg_ref, wu_ref, wd_ref, fg_ref = refs[:6]
    cast_in = refs[6:6 + n_cast]
    o_ref = refs[6 + n_cast]
    cast_out = refs[7 + n_cast:7 + 2 * n_cast]
    n_ref = refs[7 + 2 * n_cast]
    j = pl.program_id(1)

    n_chunks = x_ref.shape[0] // FFN_ROW_CHUNK

    def rows(r):
        return pl.ds(pl.multiple_of(r * FFN_ROW_CHUNK, FFN_ROW_CHUNK), FFN_ROW_CHUNK)

    def step(base_ref):
        for src, dst in zip(cast_in, cast_out):
            dst[...] = src[...].astype(dst.dtype)

        n = n_ref[...]
        tf = wg_ref.shape[1]
        acts = []
        for c in range(0, tf, FFN_SLAB):
            gate = jnp.dot(n, wg_ref[:, c:c + FFN_SLAB], preferred_element_type=jnp.float32)
            up = jnp.dot(n, wu_ref[:, c:c + FFN_SLAB], preferred_element_type=jnp.float32)
            acts.append((0.5 * jax.nn.silu(gate) * up).astype(jnp.bfloat16))
        o_ref[...] = base_ref[...] + jnp.dot(jnp.concatenate(acts, axis=1), wd_ref[...],
                                             preferred_element_type=jnp.float32)

    @pl.when(j == 0)
    def _():
        for r in range(0, x_ref.shape[0], FFN_ROW_CHUNK):
            n_ref[r:r + FFN_ROW_CHUNK, :] = _rms(x_ref[r:r + FFN_ROW_CHUNK, :],
                                                 g_ref[...]).astype(jnp.bfloat16)
        step(x_ref)

    last = pl.num_programs(1) - 1
    if not final_norm:
        @pl.when(j > 0)
        def _():
            step(o_ref)
    else:
        @pl.when((j > 0) & (j < last))
        def _():
            step(o_ref)

        @pl.when(j == last)
        def _():
            step(o_ref)
            for r in range(0, x_ref.shape[0], FFN_ROW_CHUNK):
                o_ref[r:r + FFN_ROW_CHUNK, :] = _rms(o_ref[r:r + FFN_ROW_CHUNK, :], fg_ref[...])


def _cast_plan(w, n_row_steps, n_col_steps):
    r, c = w.shape
    transpose = r > c
    assert min(r, c) % n_row_steps == 0 and max(r, c) % CAST_BLOCK_LONG == 0
    assert max(r, c) // CAST_BLOCK_LONG <= n_col_steps
    br = r // n_row_steps if not transpose else CAST_BLOCK_LONG
    bc = CAST_BLOCK_LONG if not transpose else c // n_row_steps
    if transpose:
        nlong = r // br
        return (br, bc), (lambda i, j: (jnp.minimum(j, nlong - 1), i))
    nlong = c // bc
    return (br, bc), (lambda i, j: (i, jnp.minimum(j, nlong - 1)))


def _ffn(x, gain, wg, wu, wd, fgain, *, final_norm, tm, tf, cast=()):
    t, d = x.shape
    dff = wg.shape[1]
    grid = (t // tm, dff // tf)
    plans = [_cast_plan(w, *grid) for w in cast]
    cast_specs = [pl.BlockSpec(blk, imap) for blk, imap in plans]
    outs = pl.pallas_call(
        functools.partial(_ffn_kernel, final_norm=final_norm, n_cast=len(cast)),
        grid=grid,
        in_specs=[
            pl.BlockSpec((tm, d), lambda i, j: (i, 0)),
            pl.BlockSpec((1, d), lambda i, j: (0, 0)),
            pl.BlockSpec((d, tf), lambda i, j: (0, j)),
            pl.BlockSpec((d, tf), lambda i, j: (0, j)),
            pl.BlockSpec((tf, d), lambda i, j: (j, 0)),
            pl.BlockSpec((1, d), lambda i, j: (0, 0)),
        ] + cast_specs,
        out_specs=[pl.BlockSpec((tm, d), lambda i, j: (i, 0))] + cast_specs,
        out_shape=[jax.ShapeDtypeStruct((t, d), jnp.float32)]
        + [jax.ShapeDtypeStruct(w.shape, jnp.bfloat16) for w in cast],
        scratch_shapes=[pltpu.VMEM((tm, d), jnp.bfloat16)],
        compiler_params=pltpu.CompilerParams(
            dimension_semantics=("arbitrary", "arbitrary"),
            vmem_limit_bytes=VMEM_LIMIT_BYTES),
        name=("ffn_final" if final_norm else "ffn") + ("_cast" if cast else ""),
    )(x, gain, wg, wu, wd, fgain, *cast)
    return outs[0], tuple(outs[1:])


def _rope(x, cos, sin, low_half):
    half = ROT_DIM // 2
    partner = jnp.where(low_half, pltpu.roll(x, HEAD_DIM - half, 1), pltpu.roll(x, half, 1))
    return x * cos + partner * sin


def _inproj_kernel(h_ref, g_ref, w_ref, vg_ref, cr_ref, sr_ref, ph_ref,
                   z_ref, *, gw, qw, kw):
    n = _rms(h_ref[...], g_ref[...]).astype(jnp.bfloat16)
    c1, c2, c3, c4 = gw, 2 * gw, 2 * gw + qw, 2 * gw + qw + kw
    u_ref, vn_ref, q_ref = z_ref.at[:, 0:c1], z_ref.at[:, c1:c2], z_ref.at[:, c2:c3]
    k_ref, v_ref = z_ref.at[:, c3:c4], z_ref.at[:, c4:c4 + kw]

    def proj(lo, hi):
        return jnp.dot(n, w_ref[:, lo:hi], preferred_element_type=jnp.float32)

    cr = cr_ref[...]
    sr = sr_ref[...]
    cos = ph_ref[0:1, :] * cr - ph_ref[1:2, :] * sr
    sin = ph_ref[3:4, :] * cr + ph_ref[2:3, :] * sr
    low_half = lax.broadcasted_iota(jnp.int32, cos.shape, 1) < (ROT_DIM // 2)

    zk = proj(c3, c4)
    for hd in range(kw // HEAD_DIM):
        sl = slice(hd * HEAD_DIM, (hd + 1) * HEAD_DIM)
        k_ref[:, sl] = _rope(zk[:, sl], cos, sin, low_half).astype(jnp.bfloat16)
    zq = proj(c2, c3)
    for hd in range(qw // HEAD_DIM):
        sl = slice(hd * HEAD_DIM, (hd + 1) * HEAD_DIM)
        q_ref[:, sl] = _rope(zq[:, sl], cos, sin, low_half).astype(jnp.bfloat16)
    u_ref[...] = jax.nn.gelu(proj(0, c1)).astype(jnp.bfloat16)
    vn_ref[...] = _rms(jax.nn.gelu(proj(c1, c2)), vg_ref[...]).astype(jnp.bfloat16)
    v_ref[...] = proj(c4, c4 + kw).astype(jnp.bfloat16)


def _inproj(h, gain, w_in, vgain, rope, *, seq, tm):
    t, d = h.shape
    gw = vgain.shape[1]
    qw = d - gw
    kw = qw // GQA_GROUP
    cols = w_in.shape[1]
    tiles_per_seq = seq // tm
    cr, sr, phase = rope
    row = lambda i: (i, 0)
    const = lambda i: (0, 0)
    return pl.pallas_call(
        functools.partial(_inproj_kernel, gw=gw, qw=qw, kw=kw),
        grid=(t // tm,),
        in_specs=[
            pl.BlockSpec((tm, d), row),
            pl.BlockSpec((1, d), const),
            pl.BlockSpec((d, cols), const, pipeline_mode=pl.Buffered(1)),
            pl.BlockSpec((1, gw), const),
            pl.BlockSpec((tm, HEAD_DIM), const),
            pl.BlockSpec((tm, HEAD_DIM), const),
            pl.BlockSpec((PHASE_ROWS, HEAD_DIM), lambda i: (i % tiles_per_seq, 0)),
        ],
        out_specs=pl.BlockSpec((tm, cols), row),
        out_shape=jax.ShapeDtypeStruct((t, cols), jnp.bfloat16),
        compiler_params=pltpu.CompilerParams(
            dimension_semantics=("arbitrary",),
            vmem_limit_bytes=VMEM_LIMIT_BYTES),
        name="inproj",
    )(h, gain, w_in, vgain, cr, sr, phase)


def _mixer_kernel(sink_ref, h_ref, z_ref, zp_ref, zn_ref, ws_ref, bs_ref,
                  ga_ref, gb_ref, wo_ref, o_ref,
                  a_ref, att_ref, m_ref, mprev_ref, kall_ref, vall_ref,
                  *, ntiles, tiles_per_seq, nblk, gw, qw, kw):
    s = pl.program_id(0)
    u_ref, vn_ref = z_ref.at[:, 0:gw], z_ref.at[:, gw:2 * gw]
    q_ref = z_ref.at[:, 2 * gw:2 * gw + qw]
    k_ref, v_ref = z_ref.at[:, 2 * gw + qw:2 * gw + qw + kw], z_ref.at[:, 2 * gw + qw + kw:]
    kp_ref, vp_ref = zp_ref.at[:, 0:kw], zp_ref.at[:, kw:]
    kn_ref, vn2_ref = zn_ref.at[:, 0:kw], zn_ref.at[:, kw:]
    n_kv = kw // HEAD_DIM
    rows = GQA_GROUP * CHUNK

    @pl.when(s == 0)
    def _():
        m_ref[...] = jnp.zeros_like(m_ref)

    mprev_ref[...] = m_ref[...]

    kall_ref[0:CHUNK] = kp_ref[...]
    kall_ref[CHUNK:(nblk + 1) * CHUNK] = k_ref[...]
    kall_ref[(nblk + 1) * CHUNK:] = kn_ref[...]
    vall_ref[0:CHUNK] = vp_ref[...]
    vall_ref[CHUNK:(nblk + 1) * CHUNK] = v_ref[...]
    vall_ref[(nblk + 1) * CHUNK:] = vn2_ref[...]

    t_in_seq = jnp.minimum(s, ntiles - 1) % tiles_per_seq
    prev_thr = jnp.where(t_in_seq == 0, 2 * CHUNK, 0)
    next_thr = jnp.where(t_in_seq == tiles_per_seq - 1, -2 * CHUNK, 0)
    rel = (lax.broadcasted_iota(jnp.int32, (rows, CHUNK), 1)
           - (lax.broadcasted_iota(jnp.int32, (rows, CHUNK), 0) & (CHUNK - 1)))
    log2e = 1.4426950408889634
    c2 = (HEAD_DIM ** -0.5) * log2e
    n_slabs = nblk * n_kv
    pw = o_ref.shape[1] // n_slabs
    gpi = GMLP_GROUPS // n_kv

    def project_slab(slab):
        ps = slice(slab * pw, (slab + 1) * pw)
        o_ref[:, ps] = h_ref[:, ps] + jnp.dot(mprev_ref[...], wo_ref[:, ps],
                                              preferred_element_type=jnp.float32)

    def scores(b, kh):
        qs = slice(b * CHUNK, (b + 1) * CHUNK)
        q = jnp.concatenate(
            [q_ref[qs, (kh * GQA_GROUP + g) * HEAD_DIM:(kh * GQA_GROUP + g + 1) * HEAD_DIM]
             for g in range(GQA_GROUP)], axis=0)
        kwin = kall_ref[b * CHUNK:(b + 3) * CHUNK, kh * HEAD_DIM:(kh + 1) * HEAD_DIM]
        return lax.dot_general(q, kwin, (((1,), (1,)), ((), ())),
                               preferred_element_type=jnp.float32)

    iters = [(b, kh) for b in range(nblk) for kh in range(n_kv)]
    sc_next = scores(*iters[0])
    for it, (b, kh) in enumerate(iters):
        pthr = prev_thr if b == 0 else 0
        nthr = next_thr if b == nblk - 1 else 0
        qs = slice(b * CHUNK, (b + 1) * CHUNK)
        sc = sc_next
        if it + 1 < len(iters):
            sc_next = scores(*iters[it + 1])
        if it < n_slabs - MIXER_TAIL_SLABS:
            project_slab(it)
        s_prev = jnp.where(rel >= pthr, sc[:, :CHUNK], NEG_INF)
        s_mid = sc[:, CHUNK:2 * CHUNK]
        s_next = jnp.where(rel <= nthr, sc[:, 2 * CHUNK:], NEG_INF)
        sink2 = jnp.concatenate(
            [jnp.full((CHUNK, 1), sink_ref[kh * GQA_GROUP + g] * log2e, jnp.float32)
             for g in range(GQA_GROUP)], axis=0)
        mx = jnp.maximum(
            jnp.max(jnp.maximum(jnp.maximum(s_prev, s_mid), s_next), axis=-1, keepdims=True) * c2,
            sink2)
        p_prev = jnp.exp2(s_prev * c2 - mx)
        p_mid = jnp.exp2(s_mid * c2 - mx)
        p_next = jnp.exp2(s_next * c2 - mx)
        denom = (jnp.sum(p_prev + p_mid + p_next, axis=-1, keepdims=True)
                 + jnp.exp2(sink2 - mx))
        p = jnp.concatenate([p_prev, p_mid, p_next], axis=1).astype(jnp.bfloat16)
        vwin = vall_ref[b * CHUNK:(b + 3) * CHUNK, kh * HEAD_DIM:(kh + 1) * HEAD_DIM]
        o = jnp.dot(p, vwin, preferred_element_type=jnp.float32) / denom
        for g in range(GQA_GROUP):
            hs = (kh * GQA_GROUP + g) * HEAD_DIM
            att_ref[qs, hs:hs + HEAD_DIM] = o[g * CHUNK:(g + 1) * CHUNK]
        for g in range(kh * gpi, (kh + 1) * gpi):
            gs = slice(g * CHUNK, (g + 1) * CHUNK)
            mixed = jnp.dot(ws_ref[g], vn_ref[qs, gs], preferred_element_type=jnp.float32)
            a_ref[qs, gs] = u_ref[qs, gs].astype(jnp.float32) * (mixed + bs_ref[g])
        if kh == n_kv - 1:
            m_ref[qs, :gw] = _rms(a_ref[qs, :], ga_ref[...]).astype(jnp.bfloat16)
            m_ref[qs, gw:] = _rms(att_ref[qs, :], gb_ref[...]).astype(jnp.bfloat16)
    for slab in range(n_slabs - MIXER_TAIL_SLABS, n_slabs):
        project_slab(slab)


def _mixer(h, z, sink, ws, bs, ga, gb, wo, *, seq, tm):
    t, d = h.shape
    gw = ga.shape[1]
    qw = gb.shape[1]
    kw = qw // GQA_GROUP
    cols = z.shape[1]
    kv_col_block, rem = divmod(2 * gw + qw, 2 * kw)
    assert rem == 0 and cols == 2 * gw + qw + 2 * kw
    nblk = tm // CHUNK
    nblocks = t // CHUNK
    ntiles = t // tm
    tiles_per_seq = seq // tm
    mix_tile = lambda s: jnp.minimum(s, ntiles - 1)
    mix_row = lambda s: (mix_tile(s), 0)
    proj_row = lambda s: (jnp.maximum(s - 1, 0), 0)
    const2 = lambda s: (0, 0)
    const3 = lambda s: (0, 0, 0)
    prev_blk = lambda s: (jnp.maximum(mix_tile(s) * nblk - 1, 0), kv_col_block)
    next_blk = lambda s: (jnp.minimum((mix_tile(s) + 1) * nblk, nblocks - 1), kv_col_block)
    return pl.pallas_call(
        functools.partial(_mixer_kernel, ntiles=ntiles, tiles_per_seq=tiles_per_seq, nblk=nblk,
                          gw=gw, qw=qw, kw=kw),
        grid=(ntiles + 1,),
        in_specs=[
            pl.BlockSpec(memory_space=pltpu.SMEM),
            pl.BlockSpec((tm, d), proj_row),
            pl.BlockSpec((tm, cols), mix_row),
            pl.BlockSpec((CHUNK, 2 * kw), prev_blk),
            pl.BlockSpec((CHUNK, 2 * kw), next_blk),
            pl.BlockSpec((GMLP_GROUPS, CHUNK, CHUNK), const3),
            pl.BlockSpec((GMLP_GROUPS, CHUNK, CHUNK), const3),
            pl.BlockSpec((1, gw), const2),
            pl.BlockSpec((1, qw), const2),
            pl.BlockSpec((d, d), const2, pipeline_mode=pl.Buffered(1)),
        ],
        out_specs=pl.BlockSpec((tm, d), proj_row),
        out_shape=jax.ShapeDtypeStruct((t, d), jnp.float32),
        scratch_shapes=[
            pltpu.VMEM((tm, gw), jnp.float32),
            pltpu.VMEM((tm, qw), jnp.float32),
            pltpu.VMEM((tm, d), jnp.bfloat16),
            pltpu.VMEM((tm, d), jnp.bfloat16),
            pltpu.VMEM((tm + 2 * CHUNK, kw), jnp.bfloat16),
            pltpu.VMEM((tm + 2 * CHUNK, kw), jnp.bfloat16),
        ],
        compiler_params=pltpu.CompilerParams(
            dimension_semantics=("arbitrary",),
            vmem_limit_bytes=VMEM_LIMIT_BYTES),
        name="mixer",
    )(sink, h, z, z, z, ws, bs, ga, gb, wo)


def _rope_tables(max_seq, tm):
    half = ROT_DIM // 2
    inv_freq = ROPE_THETA ** (-jnp.arange(0, ROT_DIM, 2, dtype=jnp.float32) / ROT_DIM)
    pad = jnp.zeros((HEAD_DIM - ROT_DIM,), jnp.float32)
    freq = jnp.concatenate([inv_freq, inv_freq, pad])[None, :]
    sign = jnp.concatenate([-jnp.ones((half,)), jnp.ones((half,)), pad])[None, :]
    ang_r = jnp.arange(tm, dtype=jnp.float32)[:, None] * freq
    ang_0 = jnp.arange(0, max_seq, tm, dtype=jnp.float32)[:, None] * freq
    c0, s0 = jnp.cos(ang_0), jnp.sin(ang_0)
    zero = jnp.zeros_like(c0)
    phase = jnp.stack([c0, s0, sign * c0, sign * s0] + [zero] * (PHASE_ROWS - 4), axis=1)
    return jnp.cos(ang_r), jnp.sin(ang_r), phase.reshape(-1, HEAD_DIM)


def kernel(x_prompt, x_sample, ffn1_norm, ffn1_w_gate, ffn1_w_up, ffn1_w_down, mix_norm, w_in,
           gmlp_v_norm, gmlp_w_s, gmlp_b_s, attn_sink, out_norm_gmlp, out_norm_attn, w_out,
           ffn2_norm, ffn2_w_gate, ffn2_w_up, ffn2_w_down, final_norm):
    bf = jnp.bfloat16
    assert ffn1_norm.shape[0] == 1, "one encoder layer"
    d = x_prompt.shape[-1]
    row = lambda a: a.reshape(1, -1)
    fg = row(final_norm)
    ws = gmlp_w_s[0].astype(bf)
    bs = jnp.broadcast_to(gmlp_b_s[0][:, :, None], gmlp_b_s[0].shape + (CHUNK,))
    rope = _rope_tables(max(x_prompt.shape[1], x_sample.shape[1]), INPROJ_TM)

    def flat(x):
        return x.reshape(-1, d)

    def rest_of_layer(h, seq, wg2, wu2, wd2, w_in_bf, w_out_bf):
        z = _inproj(h, row(mix_norm[0]), w_in_bf, row(gmlp_v_norm[0]), rope, seq=seq, tm=INPROJ_TM)
        h = _mixer(h, z, attn_sink[0], ws, bs, row(out_norm_gmlp[0]),
                   row(out_norm_attn[0]), w_out_bf, seq=seq, tm=MIXER_TM)
        h, _ = _ffn(h, row(ffn2_norm[0]), wg2, wu2, wd2, fg, final_norm=True, tm=FFN_TM, tf=FFN_TF)
        return h

    wg1, wu1, wd1 = (w[0].astype(bf) for w in (ffn1_w_gate, ffn1_w_up, ffn1_w_down))
    later = (ffn2_w_gate[0], ffn2_w_up[0], ffn2_w_down[0], w_in[0], w_out[0])
    h_p, later_bf = _ffn(flat(x_prompt), row(ffn1_norm[0]), wg1, wu1, wd1, fg, final_norm=False,
                         tm=FFN_TM, tf=FFN_TF, cast=later)
    h_s, _ = _ffn(flat(x_sample), row(ffn1_norm[0]), wg1, wu1, wd1, fg, final_norm=False,
                  tm=FFN_TM, tf=FFN_TF)
    y_p = rest_of_layer(h_p, x_prompt.shape[1], *later_bf)
    y_s = rest_of_layer(h_s, x_sample.shape[1], *later_bf)
    return (y_p.reshape(x_prompt.shape), y_s.reshape(x_sample.shape))
```

```python
import functools

import jax
import jax.numpy as jnp
from jax import lax
from jax.experimental import pallas as pl
from jax.experimental.pallas import tpu as pltpu

HEAD_DIM = 128
GQA_GROUP = 4
GMLP_GROUPS = 8
CHUNK = 128
ROT_DIM = HEAD_DIM // 4
ROPE_THETA = 500000.0
EPS = 1e-6
NEG_INF = -1e30

VMEM_LIMIT_BYTES = 60 * 1024 * 1024

FFN_TM = 1024
FFN_TF = 512
FFN_SLAB = 256
FFN_ROW_CHUNK = 256
CAST_BLOCK_LONG = 512
INPROJ_TM = 1024
PHASE_ROWS = 8
MIXER_TM = 512
MIXER_TAIL_SLABS = 2


def _rms(x, gain):
    return x * lax.rsqrt(jnp.mean(x * x, axis=-1, keepdims=True) + EPS) * gain


def _ffn_kernel(*refs, final_norm, n_cast):
    x_ref, g_ref, wg_ref, wu_ref, wd_ref, fg_ref = refs[:6]
    cast_in = refs[6:6 + n_cast]
    o_ref = refs[6 + n_cast]
    cast_out = refs[7 + n_cast:7 + 2 * n_cast]
    n_ref = refs[7 + 2 * n_cast]
    j = pl.program_id(1)

    def step(base_ref):
        for src, dst in zip(cast_in, cast_out):
            dst[...] = src[...].astype(dst.dtype)

        n = n_ref[...]
        tf = wg_ref.shape[1]
        acts = []
        for c in range(0, tf, FFN_SLAB):
            gate = jnp.dot(n, wg_ref[:, c:c + FFN_SLAB], preferred_element_type=jnp.float32)
            up = jnp.dot(n, wu_ref[:, c:c + FFN_SLAB], preferred_element_type=jnp.float32)
            acts.append((0.5 * jax.nn.silu(gate) * up).astype(jnp.bfloat16))
        o_ref[...] = base_ref[...] + jnp.dot(jnp.concatenate(acts, axis=1), wd_ref[...],
                                             preferred_element_type=jnp.float32)

    @pl.when(j == 0)
    def _():
        for r in range(0, x_ref.shape[0], FFN_ROW_CHUNK):
            n_ref[r:r + FFN_ROW_CHUNK, :] = _rms(x_ref[r:r + FFN_ROW_CHUNK, :],
                                                 g_ref[...]).astype(jnp.bfloat16)
        step(x_ref)

    last = pl.num_programs(1) - 1
    if not final_norm:
        @pl.when(j > 0)
        def _():
            step(o_ref)
    else:
        @pl.when((j > 0) & (j < last))
        def _():
            step(o_ref)

        @pl.when(j == last)
        def _():
            step(o_ref)
            for r in range(0, x_ref.shape[0], FFN_ROW_CHUNK):
                o_ref[r:r + FFN_ROW_CHUNK, :] = _rms(o_ref[r:r + FFN_ROW_CHUNK, :], fg_ref[...])


def _cast_plan(w, n_row_steps, n_col_steps):
    r, c = w.shape
    transpose = r > c
    assert min(r, c) % n_row_steps == 0 and max(r, c) % CAST_BLOCK_LONG == 0
    assert max(r, c) // CAST_BLOCK_LONG <= n_col_steps
    br = r // n_row_steps if not transpose else CAST_BLOCK_LONG
    bc = CAST_BLOCK_LONG if not transpose else c // n_row_steps
    if transpose:
        nlong = r // br
        return (br, bc), (lambda i, j: (jnp.minimum(j, nlong - 1), i))
    nlong = c // bc
    return (br, bc), (lambda i, j: (i, jnp.minimum(j, nlong - 1)))


def _ffn(x, gain, wg, wu, wd, fgain, *, final_norm, tm, tf, cast=()):
    t, d = x.shape
    dff = wg.shape[1]
    grid = (t // tm, dff // tf)
    plans = [_cast_plan(w, *grid) for w in cast]
    cast_specs = [pl.BlockSpec(blk, imap) for blk, imap in plans]
    outs = pl.pallas_call(
        functools.partial(_ffn_kernel, final_norm=final_norm, n_cast=len(cast)),
        grid=grid,
        in_specs=[
            pl.BlockSpec((tm, d), lambda i, j: (i, 0)),
            pl.BlockSpec((1, d), lambda i, j: (0, 0)),
            pl.BlockSpec((d, tf), lambda i, j: (0, j)),
            pl.BlockSpec((d, tf), lambda i, j: (0, j)),
            pl.BlockSpec((tf, d), lambda i, j: (j, 0)),
            pl.BlockSpec((1, d), lambda i, j: (0, 0)),
        ] + cast_specs,
        out_specs=[pl.BlockSpec((tm, d), lambda i, j: (i, 0))] + cast_specs,
        out_shape=[jax.ShapeDtypeStruct((t, d), jnp.float32)]
        + [jax.ShapeDtypeStruct(w.shape, jnp.bfloat16) for w in cast],
        scratch_shapes=[pltpu.VMEM((tm, d), jnp.bfloat16)],
        compiler_params=pltpu.CompilerParams(
            dimension_semantics=("arbitrary", "arbitrary"),
            vmem_limit_bytes=VMEM_LIMIT_BYTES),
        name=("ffn_final" if final_norm else "ffn") + ("_cast" if cast else ""),
    )(x, gain, wg, wu, wd, fgain, *cast)
    return outs[0], tuple(outs[1:])


def _rope(x, cos, sin, low_half):
    half = ROT_DIM // 2
    partner = jnp.where(low_half, pltpu.roll(x, HEAD_DIM - half, 1), pltpu.roll(x, half, 1))
    return x * cos + partner * sin


def _inproj_kernel(h_ref, g_ref, w_ref, vg_ref, cr_ref, sr_ref, ph_ref,
                   z_ref, *, gw, qw, kw):
    n = _rms(h_ref[...], g_ref[...]).astype(jnp.bfloat16)
    c1, c2, c3, c4 = gw, 2 * gw, 2 * gw + qw, 2 * gw + qw + kw
    u_ref, vn_ref, q_ref = z_ref.at[:, 0:c1], z_ref.at[:, c1:c2], z_ref.at[:, c2:c3]
    k_ref, v_ref = z_ref.at[:, c3:c4], z_ref.at[:, c4:c4 + kw]

    def proj(lo, hi):
        return jnp.dot(n, w_ref[:, lo:hi], preferred_element_type=jnp.float32)

    cr = cr_ref[...]
    sr = sr_ref[...]
    cos = ph_ref[0:1, :] * cr - ph_ref[1:2, :] * sr
    sin = ph_ref[3:4, :] * cr + ph_ref[2:3, :] * sr
    low_half = lax.broadcasted_iota(jnp.int32, cos.shape, 1) < (ROT_DIM // 2)

    zk = proj(c3, c4)
    for hd in range(kw // HEAD_DIM):
        sl = slice(hd * HEAD_DIM, (hd + 1) * HEAD_DIM)
        k_ref[:, sl] = _rope(zk[:, sl], cos, sin, low_half).astype(jnp.bfloat16)
    zq = proj(c2, c3)
    for hd in range(qw // HEAD_DIM):
        sl = slice(hd * HEAD_DIM, (hd + 1) * HEAD_DIM)
        q_ref[:, sl] = _rope(zq[:, sl], cos, sin, low_half).astype(jnp.bfloat16)
    u_ref[...] = jax.nn.gelu(proj(0, c1)).astype(jnp.bfloat16)
    vn_ref[...] = _rms(jax.nn.gelu(proj(c1, c2)), vg_ref[...]).astype(jnp.bfloat16)
    v_ref[...] = proj(c4, c4 + kw).astype(jnp.bfloat16)


def _inproj(h, gain, w_in, vgain, rope, *, seq, tm):
    t, d = h.shape
    gw = vgain.shape[1]
    qw = d - gw
    kw = qw // GQA_GROUP
    cols = w_in.shape[1]
    tiles_per_seq = seq // tm
    cr, sr, phase = rope
    row = lambda i: (i, 0)
    const = lambda i: (0, 0)
    return pl.pallas_call(
        functools.partial(_inproj_kernel, gw=gw, qw=qw, kw=kw),
        grid=(t // tm,),
        in_specs=[
            pl.BlockSpec((tm, d), row),
            pl.BlockSpec((1, d), const),
            pl.BlockSpec((d, cols), const, pipeline_mode=pl.Buffered(1)),
            pl.BlockSpec((1, gw), const),
            pl.BlockSpec((tm, HEAD_DIM), const),
            pl.BlockSpec((tm, HEAD_DIM), const),
            pl.BlockSpec((PHASE_ROWS, HEAD_DIM), lambda i: (i % tiles_per_seq, 0)),
        ],
        out_specs=pl.BlockSpec((tm, cols), row),
        out_shape=jax.ShapeDtypeStruct((t, cols), jnp.bfloat16),
        compiler_params=pltpu.CompilerParams(
            dimension_semantics=("arbitrary",),
            vmem_limit_bytes=VMEM_LIMIT_BYTES),
        name="inproj",
    )(h, gain, w_in, vgain, cr, sr, phase)


def _mixer_kernel(sink_ref, h_ref, z_ref, zp_ref, zn_ref, ws_ref, bs_ref,
                  ga_ref, gb_ref, wo_ref, o_ref,
                  a_ref, att_ref, m_ref, mprev_ref, kall_ref, vall_ref,
                  *, ntiles, tiles_per_seq, nblk, gw, qw, kw):
    s = pl.program_id(0)
    u_ref, vn_ref = z_ref.at[:, 0:gw], z_ref.at[:, gw:2 * gw]
    q_ref = z_ref.at[:, 2 * gw:2 * gw + qw]
    k_ref, v_ref = z_ref.at[:, 2 * gw + qw:2 * gw + qw + kw], z_ref.at[:, 2 * gw + qw + kw:]
    kp_ref, vp_ref = zp_ref.at[:, 0:kw], zp_ref.at[:, kw:]
    kn_ref, vn2_ref = zn_ref.at[:, 0:kw], zn_ref.at[:, kw:]
    n_kv = kw // HEAD_DIM
    rows = GQA_GROUP * CHUNK

    @pl.when(s == 0)
    def _():
        m_ref[...] = jnp.zeros_like(m_ref)

    mprev_ref[...] = m_ref[...]

    kall_ref[0:CHUNK] = kp_ref[...]
    kall_ref[CHUNK:(nblk + 1) * CHUNK] = k_ref[...]
    kall_ref[(nblk + 1) * CHUNK:] = kn_ref[...]
    vall_ref[0:CHUNK] = vp_ref[...]
    vall_ref[CHUNK:(nblk + 1) * CHUNK] = v_ref[...]
    vall_ref[(nblk + 1) * CHUNK:] = vn2_ref[...]

    t_in_seq = jnp.minimum(s, ntiles - 1) % tiles_per_seq
    prev_thr = jnp.where(t_in_seq == 0, 2 * CHUNK, 0)
    next_thr = jnp.where(t_in_seq == tiles_per_seq - 1, -2 * CHUNK, 0)
    rel = (lax.broadcasted_iota(jnp.int32, (rows, CHUNK), 1)
           - (lax.broadcasted_iota(jnp.int32, (rows, CHUNK), 0) & (CHUNK - 1)))
    log2e = 1.4426950408889634
    c2 = (HEAD_DIM ** -0.5) * log2e
    n_slabs = nblk * n_kv
    pw = o_ref.shape[1] // n_slabs
    gpi = GMLP_GROUPS // n_kv

    def project_slab(slab):
        ps = slice(slab * pw, (slab + 1) * pw)
        o_ref[:, ps] = h_ref[:, ps] + jnp.dot(mprev_ref[...], wo_ref[:, ps],
                                              preferred_element_type=jnp.float32)

    def scores(b, kh):
        qs = slice(b * CHUNK, (b + 1) * CHUNK)
        q = jnp.concatenate(
            [q_ref[qs, (kh * GQA_GROUP + g) * HEAD_DIM:(kh * GQA_GROUP + g + 1) * HEAD_DIM]
             for g in range(GQA_GROUP)], axis=0)
        kwin = kall_ref[b * CHUNK:(b + 3) * CHUNK, kh * HEAD_DIM:(kh + 1) * HEAD_DIM]
        return lax.dot_general(q, kwin, (((1,), (1,)), ((), ())),
                               preferred_element_type=jnp.float32)

    iters = [(b, kh) for b in range(nblk) for kh in range(n_kv)]
    sc_next = scores(*iters[0])
    for it, (b, kh) in enumerate(iters):
        pthr = prev_thr if b == 0 else 0
        nthr = next_thr if b == nblk - 1 else 0
        qs = slice(b * CHUNK, (b + 1) * CHUNK)
        sc = sc_next
        if it + 1 < len(iters):
            sc_next = scores(*iters[it + 1])
        if it < n_slabs - MIXER_TAIL_SLABS:
            project_slab(it)
        s_prev = jnp.where(rel >= pthr, sc[:, :CHUNK], NEG_INF)
        s_mid = sc[:, CHUNK:2 * CHUNK]
        s_next = jnp.where(rel <= nthr, sc[:, 2 * CHUNK:], NEG_INF)
        sink2 = jnp.concatenate(
            [jnp.full((CHUNK, 1), sink_ref[kh * GQA_GROUP + g] * log2e, jnp.float32)
             for g in range(GQA_GROUP)], axis=0)
        mx = jnp.maximum(
            jnp.max(jnp.maximum(jnp.maximum(s_prev, s_mid), s_next), axis=-1, keepdims=True) * c2,
            sink2)
        p_prev = jnp.exp2(s_prev * c2 - mx)
        p_mid = jnp.exp2(s_mid * c2 - mx)
        p_next = jnp.exp2(s_next * c2 - mx)
        denom = (jnp.sum(p_prev + p_mid + p_next, axis=-1, keepdims=True)
                 + jnp.exp2(sink2 - mx))
        p = jnp.concatenate([p_prev, p_mid, p_next], axis=1).astype(jnp.bfloat16)
        vwin = vall_ref[b * CHUNK:(b + 3) * CHUNK, kh * HEAD_DIM:(kh + 1) * HEAD_DIM]
        o = jnp.dot(p, vwin, preferred_element_type=jnp.float32) / denom
        for g in range(GQA_GROUP):
            hs = (kh * GQA_GROUP + g) * HEAD_DIM
            att_ref[qs, hs:hs + HEAD_DIM] = o[g * CHUNK:(g + 1) * CHUNK]
        for g in range(kh * gpi, (kh + 1) * gpi):
            gs = slice(g * CHUNK, (g + 1) * CHUNK)
            mixed = jnp.dot(ws_ref[g], vn_ref[qs, gs], preferred_element_type=jnp.float32)
            a_ref[qs, gs] = u_ref[qs, gs].astype(jnp.float32) * (mixed + bs_ref[g])
        if kh == n_kv - 1:
            m_ref[qs, :gw] = _rms(a_ref[qs, :], ga_ref[...]).astype(jnp.bfloat16)
            m_ref[qs, gw:] = _rms(att_ref[qs, :], gb_ref[...]).astype(jnp.bfloat16)
    for slab in range(n_slabs - MIXER_TAIL_SLABS, n_slabs):
        project_slab(slab)


def _mixer(h, z, sink, ws, bs, ga, gb, wo, *, seq, tm):
    t, d = h.shape
    gw = ga.shape[1]
    qw = gb.shape[1]
    kw = qw // GQA_GROUP
    cols = z.shape[1]
    kv_col_block, rem = divmod(2 * gw + qw, 2 * kw)
    assert rem == 0 and cols == 2 * gw + qw + 2 * kw
    nblk = tm // CHUNK
    nblocks = t // CHUNK
    ntiles = t // tm
    tiles_per_seq = seq // tm
    mix_tile = lambda s: jnp.minimum(s, ntiles - 1)
    mix_row = lambda s: (mix_tile(s), 0)
    proj_row = lambda s: (jnp.maximum(s - 1, 0), 0)
    const2 = lambda s: (0, 0)
    const3 = lambda s: (0, 0, 0)
    prev_blk = lambda s: (jnp.maximum(mix_tile(s) * nblk - 1, 0), kv_col_block)
    next_blk = lambda s: (jnp.minimum((mix_tile(s) + 1) * nblk, nblocks - 1), kv_col_block)
    return pl.pallas_call(
        functools.partial(_mixer_kernel, ntiles=ntiles, tiles_per_seq=tiles_per_seq, nblk=nblk,
                          gw=gw, qw=qw, kw=kw),
        grid=(ntiles + 1,),
        in_specs=[
            pl.BlockSpec(memory_space=pltpu.SMEM),
            pl.BlockSpec((tm, d), proj_row),
            pl.BlockSpec((tm, cols), mix_row),
            pl.BlockSpec((CHUNK, 2 * kw), prev_blk),
            pl.BlockSpec((CHUNK, 2 * kw), next_blk),
            pl.BlockSpec((GMLP_GROUPS, CHUNK, CHUNK), const3),
            pl.BlockSpec((GMLP_GROUPS, CHUNK, CHUNK), const3),
            pl.BlockSpec((1, gw), const2),
            pl.BlockSpec((1, qw), const2),
            pl.BlockSpec((d, d), const2, pipeline_mode=pl.Buffered(1)),
        ],
        out_specs=pl.BlockSpec((tm, d), proj_row),
        out_shape=jax.ShapeDtypeStruct((t, d), jnp.float32),
        scratch_shapes=[
            pltpu.VMEM((tm, gw), jnp.float32),
            pltpu.VMEM((tm, qw), jnp.float32),
            pltpu.VMEM((tm, d), jnp.bfloat16),
            pltpu.VMEM((tm, d), jnp.bfloat16),
            pltpu.VMEM((tm + 2 * CHUNK, kw), jnp.bfloat16),
            pltpu.VMEM((tm + 2 * CHUNK, kw), jnp.bfloat16),
        ],
        compiler_params=pltpu.CompilerParams(
            dimension_semantics=("arbitrary",),
            vmem_limit_bytes=VMEM_LIMIT_BYTES),
        name="mixer",
    )(sink, h, z, z, z, ws, bs, ga, gb, wo)


def _rope_tables(max_seq, tm):
    half = ROT_DIM // 2
    inv_freq = ROPE_THETA ** (-jnp.arange(0, ROT_DIM, 2, dtype=jnp.float32) / ROT_DIM)
    pad = jnp.zeros((HEAD_DIM - ROT_DIM,), jnp.float32)
    freq = jnp.concatenate([inv_freq, inv_freq, pad])[None, :]
    sign = jnp.concatenate([-jnp.ones((half,)), jnp.ones((half,)), pad])[None, :]
    ang_r = jnp.arange(tm, dtype=jnp.float32)[:, None] * freq
    ang_0 = jnp.arange(0, max_seq, tm, dtype=jnp.float32)[:, None] * freq
    c0, s0 = jnp.cos(ang_0), jnp.sin(ang_0)
    zero = jnp.zeros_like(c0)
    phase = jnp.stack([c0, s0, sign * c0, sign * s0] + [zero] * (PHASE_ROWS - 4), axis=1)
    return jnp.cos(ang_r), jnp.sin(ang_r), phase.reshape(-1, HEAD_DIM)


def kernel(x_prompt, x_sample, ffn1_norm, ffn1_w_gate, ffn1_w_up, ffn1_w_down, mix_norm, w_in,
           gmlp_v_norm, gmlp_w_s, gmlp_b_s, attn_sink, out_norm_gmlp, out_norm_attn, w_out,
           ffn2_norm, ffn2_w_gate, ffn2_w_up, ffn2_w_down, final_norm):
    bf = jnp.bfloat16
    assert ffn1_norm.shape[0] == 1, "one encoder layer"
    d = x_prompt.shape[-1]
    row = lambda a: a.reshape(1, -1)
    fg = row(final_norm)
    ws = gmlp_w_s[0].astype(bf)
    bs = jnp.broadcast_to(gmlp_b_s[0][:, :, None], gmlp_b_s[0].shape + (CHUNK,))
    rope = _rope_tables(max(x_prompt.shape[1], x_sample.shape[1]), INPROJ_TM)

    def flat(x):
        return x.reshape(-1, d)

    def rest_of_layer(h, seq, wg2, wu2, wd2, w_in_bf, w_out_bf):
        z = _inproj(h, row(mix_norm[0]), w_in_bf, row(gmlp_v_norm[0]), rope, seq=seq, tm=INPROJ_TM)
        h = _mixer(h, z, attn_sink[0], ws, bs, row(out_norm_gmlp[0]),
                   row(out_norm_attn[0]), w_out_bf, seq=seq, tm=MIXER_TM)
        h, _ = _ffn(h, row(ffn2_norm[0]), wg2, wu2, wd2, fg, final_norm=True, tm=FFN_TM, tf=FFN_TF)
        return h

    wg1, wu1, wd1 = (w[0].astype(bf) for w in (ffn1_w_gate, ffn1_w_up, ffn1_w_down))
    later = (ffn2_w_gate[0], ffn2_w_up[0], ffn2_w_down[0], w_in[0], w_out[0])
    h_p, later_bf = _ffn(flat(x_prompt), row(ffn1_norm[0]), wg1, wu1, wd1, fg, final_norm=False,
                         tm=FFN_TM, tf=FFN_TF, cast=later)
    h_s, _ = _ffn(flat(x_sample), row(ffn1_norm[0]), wg1, wu1, wd1, fg, final_norm=False,
                  tm=FFN_TM, tf=FFN_TF)
    y_p = rest_of_layer(h_p, x_prompt.shape[1], *later_bf)
    y_s = rest_of_layer(h_s, x_sample.shape[1], *later_bf)
    return (y_p.reshape(x_prompt.shape), y_s.reshape(x_sample.shape))
```
